```python
import math
import jax, jax.numpy as jnp
from jax import lax
import numpy as np

D_MODEL = 1024
BATCH = 16
SEQ = 4096
DEPTH = 1

CHUNK = 64
D_MIX = D_MODEL
S5_WIDTH = D_MIX // 2
S5_GROUP = 16
S5_GROUPS = S5_WIDTH // S5_GROUP
S5_STATE = 64
S5_DT_MIN = 0.001
S5_DT_MAX = 0.1
RET_WIDTH = D_MIX - S5_WIDTH
RET_HEADS = 4
RET_DV = RET_WIDTH // RET_HEADS
RET_DK = RET_DV // 2
D_IN = S5_WIDTH + 2 * RET_HEADS * RET_DK + 2 * RET_WIDTH
D_FF = 2816
CONV_W = 3
ROPE_BASE = 10000.0
LN_EPS = 1e-5
DEEPNORM_ALPHA = (2.0 * DEPTH) ** 0.25
DEEPNORM_BETA = (8.0 * DEPTH) ** -0.25

kernel_name = "hymba_s5_retnet_convffn_deepnorm"


def layer_norm(x, g, b):
    x32 = x.astype(jnp.float32)
    mu = jnp.mean(x32, axis=-1, keepdims=True)
    xc = x32 - mu
    var = jnp.mean(jnp.square(xc), axis=-1, keepdims=True)
    y = xc * lax.rsqrt(var + LN_EPS) * g.astype(jnp.float32) + b.astype(jnp.float32)
    return y.astype(x.dtype)


def rms_norm(x, g):
    x32 = x.astype(jnp.float32)
    y = x32 * lax.rsqrt(jnp.mean(jnp.square(x32), axis=-1, keepdims=True) + LN_EPS)
    return y * g.astype(jnp.float32)


def rotary(x, pos):
    half = x.shape[-1] // 2
    freqs = ROPE_BASE ** (-jnp.arange(half, dtype=jnp.float32) / half)
    ang = pos[:, None] * freqs[None, :]
    cos = jnp.cos(ang)[None, :, None, :]
    sin = jnp.sin(ang)[None, :, None, :]
    x1, x2 = x[..., :half], x[..., half:]
    return jnp.concatenate([x1 * cos - x2 * sin, x1 * sin + x2 * cos], axis=-1)


def _ssm_combine(left, right):
    a_l, b_l = left
    a_r, b_r = right
    return a_r * a_l, a_r * b_l + b_r


def s5_mixer(u, lam_re, lam_im, b_re, b_im, c_re, c_im, d, log_step, glu_w, glu_b):
    bsz, seq_len, _ = u.shape
    f32 = jnp.float32
    u32 = u.astype(f32).reshape(bsz, seq_len, S5_GROUPS, S5_GROUP)
    lam = lax.complex(lam_re.astype(f32), lam_im.astype(f32))
    dt = jnp.exp(log_step.astype(f32))[:, None]
    lam_bar = jnp.exp(lam * dt)
    b = lax.complex(b_re.astype(f32), b_im.astype(f32))
    b_bar = ((lam_bar - 1.0) / lam)[..., None] * b
    bu = jnp.einsum('blgc,gpc->blgp', u32.astype(jnp.complex64), b_bar)
    a_elems = jnp.broadcast_to(lam_bar[None, None], (1, seq_len, S5_GROUPS, S5_STATE))
    _, states = lax.associative_scan(_ssm_combine, (a_elems, bu), axis=1)
    c = lax.complex(c_re.astype(f32), c_im.astype(f32))
    y = jnp.real(jnp.einsum('blgp,gcp->blgc', states, c))
    y = y + d.astype(f32).reshape(S5_GROUPS, S5_GROUP) * u32
    y = jax.nn.gelu(y.reshape(bsz, seq_len, S5_WIDTH))
    return y * jax.nn.sigmoid(y @ glu_w.astype(f32) + glu_b.astype(f32))


def retention(q, k, v, gn_gain):
    bsz, seq_len = q.shape[0], q.shape[1]
    n_chunks = seq_len // CHUNK
    f32 = jnp.float32
    log_gamma = jnp.log1p(-(2.0 ** (-5.0 - jnp.arange(RET_HEADS, dtype=f32))))
    k = k * (RET_DK ** -0.5)
    qc = q.reshape(bsz, n_chunks, CHUNK, RET_HEADS, RET_DK)
    kc = k.reshape(bsz, n_chunks, CHUNK, RET_HEADS, RET_DK)
    vc = v.reshape(bsz, n_chunks, CHUNK, RET_HEADS, RET_DV)
    pos = jnp.arange(CHUNK, dtype=f32)
    intra_decay = jnp.exp(jnp.abs(pos[:, None] - pos[None, :])[None] * log_gamma[:, None, None])
    scores = jnp.einsum('bncht,bnmht->bhncm', qc, kc) * intra_decay[:, None]
    intra = jnp.einsum('bhncm,bnmhe->bnche', scores, vc)
    zeta = jnp.exp((CHUNK - 1.0 - pos)[None, :] * log_gamma[:, None])
    xi = jnp.exp((pos + 1.0)[None, :] * log_gamma[:, None])
    kv = jnp.einsum('bnmht,hm,bnmhe->nbhte', kc, zeta, vc)
    chunk_decay = jnp.exp(CHUNK * log_gamma)[:, None, None]

    def step(state, kv_n):
        return state * chunk_decay + kv_n, state

    init = jnp.zeros((bsz, RET_HEADS, RET_DK, RET_DV), f32)
    _, states_before = lax.scan(step, init, kv)
    cross = jnp.einsum('bncht,hc,nbhte->bnche', qc, xi, states_before)
    out = intra + cross
    mu = jnp.mean(out, axis=-1, keepdims=True)
    var = jnp.mean(jnp.square(out - mu), axis=-1, keepdims=True)
    out = (out - mu) * lax.rsqrt(var + LN_EPS) * gn_gain.astype(f32).reshape(RET_HEADS, RET_DV)
    return out.reshape(bsz, seq_len, RET_WIDTH)


def hybrid_mixer(x, w_in, lam_re, lam_im, b_re, b_im, c_re, c_im, d, log_step,
                 glu_w, glu_b, s5_gain, gn_gain, w_out):
    bsz, seq_len, _ = x.shape
    h = x @ w_in
    o1 = S5_WIDTH
    o2 = o1 + RET_HEADS * RET_DK
    o3 = o2 + RET_HEADS * RET_DK
    o4 = o3 + RET_WIDTH
    u_s5 = h[..., :o1]
    pos = jnp.arange(seq_len, dtype=jnp.float32)
    q = rotary(h[..., o1:o2].astype(jnp.float32).reshape(bsz, seq_len, RET_HEADS, RET_DK), pos)
    k = rotary(h[..., o2:o3].astype(jnp.float32).reshape(bsz, seq_len, RET_HEADS, RET_DK), pos)
    v = h[..., o3:o4].astype(jnp.float32).reshape(bsz, seq_len, RET_HEADS, RET_DV)
    gate = h[..., o4:].astype(jnp.float32)
    y_s5 = rms_norm(s5_mixer(u_s5, lam_re, lam_im, b_re, b_im, c_re, c_im, d, log_step, glu_w, glu_b), s5_gain)
    y_ret = retention(q, k, v, gn_gain) * jax.nn.silu(gate)
    y = jnp.concatenate([y_s5, y_ret], axis=-1).astype(x.dtype)
    return y @ w_out


def conv_ffn(x, w_up, conv_w, conv_b, w_down):
    seq_len = x.shape[1]
    ug = x @ w_up
    a, g = ug[..., :D_FF], ug[..., D_FF:]
    a_pad = jnp.pad(a, ((0, 0), (CONV_W - 1, 0), (0, 0)))
    a = sum(conv_w[i] * a_pad[:, i:i + seq_len] for i in range(CONV_W)) + conv_b
    return (jax.nn.silu(a) * g) @ w_down


def setup_inputs(seed: int = 0) -> dict:
    key = jax.random.key(seed)
    ks = jax.random.split(key, 24)
    f32 = jnp.float32
    nrm = lambda k, shape, s: jax.random.normal(k, shape, f32) * s
    x = jax.random.normal(ks[0], (BATCH, SEQ, D_MODEL), f32)
    col_scale = jnp.concatenate([
        jnp.ones((S5_WIDTH + 2 * RET_HEADS * RET_DK,), f32),
        jnp.full((RET_WIDTH,), DEEPNORM_BETA, f32),
        jnp.ones((RET_WIDTH,), f32)])
    w_in = nrm(ks[1], (DEPTH, D_MODEL, D_IN), D_MODEL ** -0.5) * col_scale
    s5_lambda_re = -0.5 + nrm(ks[2], (DEPTH, S5_GROUPS, S5_STATE), 0.01)
    s5_lambda_im = math.pi * jnp.arange(S5_STATE, dtype=f32)[None, None, :] + nrm(ks[3], (DEPTH, S5_GROUPS, S5_STATE), 0.01)
    s5_b_re = nrm(ks[4], (DEPTH, S5_GROUPS, S5_STATE, S5_GROUP), (2.0 * S5_GROUP) ** -0.5)
    s5_b_im = nrm(ks[5], (DEPTH, S5_GROUPS, S5_STATE, S5_GROUP), (2.0 * S5_GROUP) ** -0.5)
    s5_c_re = nrm(ks[6], (DEPTH, S5_GROUPS, S5_GROUP, S5_STATE), (2.0 * S5_STATE) ** -0.5)
    s5_c_im = nrm(ks[7], (DEPTH, S5_GROUPS, S5_GROUP, S5_STATE), (2.0 * S5_STATE) ** -0.5)
    s5_d = nrm(ks[8], (DEPTH, S5_WIDTH), 1.0)
    s5_log_step = jax.random.uniform(ks[9], (DEPTH, S5_GROUPS), f32, math.log(S5_DT_MIN), math.log(S5_DT_MAX))
    s5_glu_w = nrm(ks[10], (DEPTH, S5_WIDTH, S5_WIDTH), S5_WIDTH ** -0.5)
    s5_glu_b = nrm(ks[11], (DEPTH, S5_WIDTH), 0.01)
    s5_out_gain = 1.0 + nrm(ks[12], (DEPTH, S5_WIDTH), 0.01)
    ret_gn_gain = 1.0 + nrm(ks[13], (DEPTH, RET_WIDTH), 0.01)
    w_out = nrm(ks[14], (DEPTH, D_MIX, D_MODEL), D_MIX ** -0.5 * DEEPNORM_BETA)
    ln1_g = 1.0 + nrm(ks[15], (DEPTH, D_MODEL), 0.01)
    ln1_b = nrm(ks[16], (DEPTH, D_MODEL), 0.01)
    ffn_w_up = nrm(ks[17], (DEPTH, D_MODEL, 2 * D_FF), D_MODEL ** -0.5)
    ffn_conv_w = nrm(ks[18], (DEPTH, CONV_W, D_FF), CONV_W ** -0.5)
    ffn_conv_b = nrm(ks[19], (DEPTH, D_FF), 0.01)
    ffn_w_down = nrm(ks[20], (DEPTH, D_FF, D_MODEL), D_FF ** -0.5 * DEEPNORM_BETA)
    ln2_g = 1.0 + nrm(ks[21], (DEPTH, D_MODEL), 0.01)
    ln2_b = nrm(ks[22], (DEPTH, D_MODEL), 0.01)
    return {"x": x, "w_in": w_in, "s5_lambda_re": s5_lambda_re, "s5_lambda_im": s5_lambda_im,
            "s5_b_re": s5_b_re, "s5_b_im": s5_b_im, "s5_c_re": s5_c_re, "s5_c_im": s5_c_im,
            "s5_d": s5_d, "s5_log_step": s5_log_step, "s5_glu_w": s5_glu_w, "s5_glu_b": s5_glu_b,
            "s5_out_gain": s5_out_gain, "ret_gn_gain": ret_gn_gain, "w_out": w_out,
            "ln1_g": ln1_g, "ln1_b": ln1_b, "ffn_w_up": ffn_w_up, "ffn_conv_w": ffn_conv_w,
            "ffn_conv_b": ffn_conv_b, "ffn_w_down": ffn_w_down, "ln2_g": ln2_g, "ln2_b": ln2_b}


def reference(x, w_in, s5_lambda_re, s5_lambda_im, s5_b_re, s5_b_im, s5_c_re, s5_c_im,
              s5_d, s5_log_step, s5_glu_w, s5_glu_b, s5_out_gain, ret_gn_gain, w_out,
              ln1_g, ln1_b, ffn_w_up, ffn_conv_w, ffn_conv_b, ffn_w_down, ln2_g, ln2_b):
    for l in range(DEPTH):
        mix = hybrid_mixer(x, w_in[l], s5_lambda_re[l], s5_lambda_im[l], s5_b_re[l], s5_b_im[l],
                           s5_c_re[l], s5_c_im[l], s5_d[l], s5_log_step[l], s5_glu_w[l], s5_glu_b[l],
                           s5_out_gain[l], ret_gn_gain[l], w_out[l])
        x = layer_norm(DEEPNORM_ALPHA * x + mix, ln1_g[l], ln1_b[l])
        ffn = conv_ffn(x, ffn_w_up[l], ffn_conv_w[l], ffn_conv_b[l], ffn_w_down[l])
        x = layer_norm(DEEPNORM_ALPHA * x + ffn, ln2_g[l], ln2_b[l])
    return x
```

```python
import functools
import math

import numpy as np
import jax
import jax.numpy as jnp
from jax import lax
from jax.experimental import pallas as pl
from jax.experimental.pallas import tpu as pltpu

CHUNK = 64
S5_GROUP = 16
S5_STATE = 64
RET_HEADS = 4
RET_DK = 64
RET_DV = 128
CONV_W = 3
ROPE_BASE = 10000.0
LN_EPS = 1e-5

LANES = 128
SUBLANES_BF16 = 16
VMEM_LIMIT_BYTES = 56 * 1024 * 1024

TL_FRONT = 512
TB_RET = 256
S5_Q = 8
S5_NB = 32
TM_FFN = 512
FC_FFN = 256

F32 = jnp.float32
BF16 = jnp.bfloat16
HI = lax.Precision.HIGHEST


def _const_spec(shape):
    nd = len(shape)
    return pl.BlockSpec(shape, lambda *_: (0,) * nd, pipeline_mode=pl.Buffered(1))


def _layer_norm(v, g, b):
    mu = jnp.mean(v, axis=-1, keepdims=True)
    vc = v - mu
    var = jnp.mean(vc * vc, axis=-1, keepdims=True)
    return vc * lax.rsqrt(var + LN_EPS) * g + b


def _sigmoid(v):
    return 1.0 / (1.0 + jnp.exp(-v))


def _front_kernel(x_ref, w_ref, cos_ref, sin_ref, hm_ref, dmask_ref, xi_ref, zk_ref,
                  gdec_ref, gn_ref, u_ref, yr_ref, state_ref):
    @pl.when(pl.program_id(1) == 0)
    def _():
        state_ref[...] = jnp.zeros_like(state_ref)

    s5w = u_ref.shape[-1]
    xb = x_ref[0].astype(BF16)
    u = jnp.dot(xb, w_ref[:, 0:s5w], preferred_element_type=F32)
    u_ref[0] = u.astype(BF16)

    o_qk = s5w
    qk = jnp.dot(xb, w_ref[:, o_qk:o_qk + 4 * LANES], preferred_element_type=F32)
    cos = cos_ref[...]
    sin = sin_ref[...]
    q1, q2 = qk[:, 0:LANES], qk[:, LANES:2 * LANES]
    k1, k2 = qk[:, 2 * LANES:3 * LANES], qk[:, 3 * LANES:4 * LANES]
    qf = jnp.concatenate([q1 * cos - q2 * sin, q1 * sin + q2 * cos], axis=1)
    kf = jnp.concatenate([k1 * cos - k2 * sin, k1 * sin + k2 * cos], axis=1)

    o_v = o_qk + 4 * LANES
    ret_w = RET_HEADS * RET_DV
    v = jnp.dot(xb, w_ref[:, o_v:o_v + ret_w], preferred_element_type=F32).astype(BF16)
    gate = jnp.dot(xb, w_ref[:, o_v + ret_w:o_v + 2 * ret_w], preferred_element_type=F32)

    tl = xb.shape[0]
    for blk in range(tl // TB_RET):
        r0 = blk * TB_RET
        qb = qf[r0:r0 + TB_RET].astype(BF16)
        kb = kf[r0:r0 + TB_RET]
        for hd in range(RET_HEADS):
            c0 = hd * RET_DV
            kh = (kb * hm_ref[hd]).astype(BF16)
            sc = lax.dot_general(qb, kh, (((1,), (1,)), ((), ())), preferred_element_type=F32)
            p = (sc * dmask_ref[hd]).astype(BF16)
            vh = v[r0:r0 + TB_RET, c0:c0 + RET_DV]
            st = state_ref[hd]
            o = jnp.dot(p, vh, preferred_element_type=F32)
            o = o + xi_ref[hd] * jnp.dot(qb, st.astype(BF16), preferred_element_type=F32)
            kz = (kb * zk_ref[hd]).astype(BF16)
            upd = lax.dot_general(kz, vh, (((0,), (0,)), ((), ())), preferred_element_type=F32)
            state_ref[hd] = st * gdec_ref[hd] + upd
            mu = jnp.mean(o, axis=-1, keepdims=True)
            oc = o - mu
            var = jnp.mean(oc * oc, axis=-1, keepdims=True)
            on = oc * lax.rsqrt(var + LN_EPS) * gn_ref[:, c0:c0 + RET_DV]
            g = gate[r0:r0 + TB_RET, c0:c0 + RET_DV]
            yr_ref[0, r0:r0 + TB_RET, c0:c0 + RET_DV] = (on * (g * _sigmoid(g))).astype(BF16)


def _retention_consts(tb):
    log_gamma = jnp.log1p(-(2.0 ** (-5.0 - jnp.arange(RET_HEADS, dtype=F32))))
    pos = jnp.arange(tb, dtype=F32)
    chunk = jnp.arange(tb) // CHUNK
    visible = (chunk[None, :] <= chunk[:, None]).astype(F32)
    dist = jnp.abs(pos[:, None] - pos[None, :])
    dmask = jnp.exp(dist[None] * log_gamma[:, None, None]) * visible[None]
    xi = jnp.exp((pos + 1.0)[None, :] * log_gamma[:, None])
    zeta = jnp.exp((tb - 1.0 - pos)[None, :] * log_gamma[:, None])
    gdec = jnp.exp(tb * log_gamma)
    lane = np.arange(2 * LANES) % LANES
    half = RET_DK // 2
    hm = np.stack([(lane // half == h) for h in range(RET_HEADS)]).astype(np.float32)
    hm = jnp.asarray(hm) * (RET_DK ** -0.5)
    xi_b = jnp.broadcast_to(xi[:, :, None], (RET_HEADS, tb, RET_DV))
    zk = zeta[:, :, None] * hm[:, None, :]
    gdec_b = jnp.broadcast_to(gdec[:, None, None], (RET_HEADS, 2 * LANES, RET_DV))
    return hm[:, None, :], dmask, xi_b, zk, gdec_b


def _front_weight(w_in, s5w):
    half = RET_DK // 2
    first = np.concatenate([np.arange(h * RET_DK, h * RET_DK + half) for h in range(RET_HEADS)])
    second = first + half
    nqk = RET_HEADS * RET_DK
    perm = np.concatenate([np.arange(s5w), s5w + first, s5w + second, s5w + nqk + first,
                           s5w + nqk + second, np.arange(s5w + 2 * nqk, w_in.shape[1])])
    return w_in[:, perm].astype(BF16)


def _mixer_front(x, w_in, gn_gain, s5w):
    bsz, seq, d = x.shape
    tl = TL_FRONT
    ret_w = RET_HEADS * RET_DV
    half = RET_DK // 2
    freqs = ROPE_BASE ** (-jnp.arange(half, dtype=F32) / half)
    ang = jnp.arange(seq, dtype=F32)[:, None] * freqs[None, :]
    cos = jnp.tile(jnp.cos(ang), (1, RET_HEADS))
    sin = jnp.tile(jnp.sin(ang), (1, RET_HEADS))
    hm, dmask, xi_b, zk, gdec_b = _retention_consts(TB_RET)
    w = _front_weight(w_in, s5w)
    return pl.pallas_call(
        _front_kernel,
        grid=(bsz, seq // tl),
        in_specs=[
            pl.BlockSpec((1, tl, d), lambda b, i: (b, i, 0)),
            _const_spec(w.shape),
            pl.BlockSpec((tl, LANES), lambda b, i: (i, 0)),
            pl.BlockSpec((tl, LANES), lambda b, i: (i, 0)),
            _const_spec(hm.shape), _const_spec(dmask.shape), _const_spec(xi_b.shape),
            _const_spec(zk.shape), _const_spec(gdec_b.shape), _const_spec((1, ret_w)),
        ],
        out_specs=[
            pl.BlockSpec((1, tl, s5w), lambda b, i: (b, i, 0)),
            pl.BlockSpec((1, tl, ret_w), lambda b, i: (b, i, 0)),
        ],
        out_shape=[jax.ShapeDtypeStruct((bsz, seq, s5w), BF16),
                   jax.ShapeDtypeStruct((bsz, seq, ret_w), BF16)],
        scratch_shapes=[pltpu.VMEM((RET_HEADS, 2 * LANES, RET_DV), F32)],
        compiler_params=pltpu.CompilerParams(
            dimension_semantics=("arbitrary", "arbitrary"), vmem_limit_bytes=VMEM_LIMIT_BYTES),
        name="mixer_front",
    )(x, w, cos, sin, hm, dmask, xi_b, zk, gdec_b, gn_gain.reshape(1, ret_w).astype(F32))


def _s5_kernel(u_ref, t_ref, e_ref, c_ref, aq_ref, y_ref, sprev_ref, carry_ref):
    @pl.when(pl.program_id(1) == 0)
    def _():
        carry_ref[...] = jnp.zeros_like(carry_ref)

    nb, q, bsz, _ = u_ref.shape
    xcat = jnp.concatenate([u_ref[:, j].reshape(nb * bsz, LANES) for j in range(q)], axis=1)
    e = jnp.dot(xcat, e_ref[0], preferred_element_type=F32)
    ns = e.shape[1] // 2
    aqr = aq_ref[0, :, 0:ns]
    aqi = aq_ref[0, :, ns:2 * ns]
    s_re = carry_ref[:, 0:ns]
    s_im = carry_ref[:, ns:2 * ns]
    for n in range(nb):
        r0 = n * bsz
        sprev_ref[r0:r0 + bsz, :] = jnp.concatenate([s_re, s_im], axis=1).astype(BF16)
        e_re = e[r0:r0 + bsz, 0:ns]
        e_im = e[r0:r0 + bsz, ns:2 * ns]
        s_re, s_im = aqr * s_re - aqi * s_im + e_re, aqr * s_im + aqi * s_re + e_im
    carry_ref[...] = jnp.concatenate([s_re, s_im], axis=1)
    y = jnp.dot(xcat, t_ref[0], preferred_element_type=F32)
    y = y + jnp.dot(sprev_ref[...], c_ref[0], preferred_element_type=F32)
    for j in range(q):
        y_ref[:, j] = y[:, j * LANES:(j + 1) * LANES].reshape(nb, bsz, LANES).astype(BF16)


def _s5_matrices(lam_re, lam_im, b_re, b_im, c_re, c_im, d, log_step, q):
    g, p = lam_re.shape
    cch = b_re.shape[-1]
    gpl = LANES // cch
    nblk = g // gpl
    dt = jnp.exp(log_step)[:, None]
    zr, zi = lam_re * dt, lam_im * dt
    ks = jnp.arange(q + 1, dtype=F32)[:, None, None]
    mag = jnp.exp(ks * zr[None])
    pr, pi = mag * jnp.cos(ks * zi[None]), mag * jnp.sin(ks * zi[None])
    nr, ni = pr[1] - 1.0, pi[1]
    den = lam_re * lam_re + lam_im * lam_im
    fr = (nr * lam_re + ni * lam_im) / den
    fi = (ni * lam_re - nr * lam_im) / den
    bbr = fr[..., None] * b_re - fi[..., None] * b_im
    bbi = fr[..., None] * b_im + fi[..., None] * b_re
    wr = pr[:q, :, :, None] * bbr[None] - pi[:q, :, :, None] * bbi[None]
    wi = pr[:q, :, :, None] * bbi[None] + pi[:q, :, :, None] * bbr[None]
    kern = (jnp.einsum('gop,kgpc->kgoc', c_re, wr, precision=HI)
            - jnp.einsum('gop,kgpc->kgoc', c_im, wi, precision=HI))
    kern = kern.at[0].add(jnp.eye(cch, dtype=F32)[None] * d.reshape(g, 1, cch))
    lag = np.arange(q)[None, :] - np.arange(q)[:, None]
    tg = kern[np.clip(lag, 0, q - 1)] * jnp.asarray(lag >= 0, F32)[:, :, None, None, None]
    eye = jnp.eye(gpl, dtype=F32)
    tg = tg.reshape(q, q, nblk, gpl, cch, cch)
    tmat = jnp.einsum('ijmgoc,gh->migcjho', tg, eye).reshape(nblk, q * LANES, q * LANES)
    er = wr[::-1].reshape(q, nblk, gpl, p, cch)
    ei = wi[::-1].reshape(q, nblk, gpl, p, cch)
    emat = jnp.concatenate([jnp.einsum('imgpc,gh->migchp', er, eye).reshape(nblk, q * LANES, gpl * p),
                            jnp.einsum('imgpc,gh->migchp', ei, eye).reshape(nblk, q * LANES, gpl * p)], axis=2)
    cr = c_re[None] * pr[1:, :, None, :] - c_im[None] * pi[1:, :, None, :]
    ci = c_re[None] * pi[1:, :, None, :] + c_im[None] * pr[1:, :, None, :]
    cr = cr.reshape(q, nblk, gpl, cch, p)
    ci = ci.reshape(q, nblk, gpl, cch, p)
    cmat = jnp.concatenate([jnp.einsum('jmgop,gh->mgpjho', cr, eye).reshape(nblk, gpl * p, q * LANES),
                            -jnp.einsum('jmgop,gh->mgpjho', ci, eye).reshape(nblk, gpl * p, q * LANES)], axis=1)
    aq = jnp.concatenate([pr[q].reshape(nblk, gpl * p), pi[q].reshape(nblk, gpl * p)], axis=1)
    return tmat.astype(BF16), emat.astype(BF16), cmat.astype(BF16), aq


def _s5_scan(u, mats):
    tmat, emat, cmat, aq = mats
    bsz, seq, w = u.shape
    q, nb = S5_Q, S5_NB
    nblk = w // LANES
    u4 = jnp.transpose(u, (1, 0, 2)).reshape(seq // q, q, bsz, w)
    aq_b = jnp.broadcast_to(aq[:, None, :], (nblk, bsz, aq.shape[-1]))
    blk = pl.BlockSpec((nb, q, bsz, LANES), lambda m, i: (i, 0, 0, m))
    mat = lambda a: pl.BlockSpec((1,) + a.shape[1:], lambda m, i: (m, 0, 0))
    y4 = pl.pallas_call(
        _s5_kernel,
        grid=(nblk, seq // (q * nb)),
        in_specs=[blk, mat(tmat), mat(emat), mat(cmat), mat(aq_b)],
        out_specs=blk,
        out_shape=jax.ShapeDtypeStruct(u4.shape, BF16),
        scratch_shapes=[pltpu.VMEM((nb * bsz, cmat.shape[1]), BF16),
                        pltpu.VMEM((bsz, aq.shape[-1]), F32)],
        compiler_params=pltpu.CompilerParams(
            dimension_semantics=("arbitrary", "arbitrary"), vmem_limit_bytes=VMEM_LIMIT_BYTES),
        name="s5_scan",
    )(u4, tmat, emat, cmat, aq_b)
    return jnp.transpose(y4.reshape(seq, bsz, w), (1, 0, 2))


def _ffn_kernel(alpha, x_ref, ys_ref, yr_ref, gluw_ref, glub_ref, s5g_ref, wout_ref, ln1g_ref, ln1b_ref,
                wup_ref, cw_ref, cb_ref, wdn_ref, ln2g_ref, ln2b_ref, o_ref, hmid_ref, tail_ref):
    @pl.when(pl.program_id(1) == 0)
    def _():
        tail_ref[...] = jnp.zeros_like(tail_ref)

    x = x_ref[0]
    tm = x.shape[0]
    s5w = ys_ref.shape[-1]
    y = ys_ref[0].astype(F32)
    cdf = 0.5 * (1.0 + jnp.tanh(math.sqrt(2.0 / math.pi) * (y + 0.044715 * (y * y * y))))
    y = y * cdf
    z = jnp.dot(y.astype(BF16), gluw_ref[...], preferred_element_type=F32) + glub_ref[...]
    y = y * _sigmoid(z)
    y = y * lax.rsqrt(jnp.mean(y * y, axis=-1, keepdims=True) + LN_EPS) * s5g_ref[...]
    mix = jnp.dot(y.astype(BF16), wout_ref[0:s5w, :], preferred_element_type=F32)
    mix = mix + jnp.dot(yr_ref[0], wout_ref[s5w:, :], preferred_element_type=F32)
    x1 = _layer_norm(alpha * x + mix, ln1g_ref[...], ln1b_ref[...])

    xb = x1.astype(BF16)
    dff = hmid_ref.shape[1]
    halo = tail_ref.shape[0]
    for c in range(dff // FC_FFN):
        c0 = c * FC_FFN
        a = jnp.dot(xb, wup_ref[:, c0:c0 + FC_FFN], preferred_element_type=F32)
        g = jnp.dot(xb, wup_ref[:, dff + c0:dff + c0 + FC_FFN], preferred_element_type=F32)
        ext = jnp.concatenate([tail_ref[:, c0:c0 + FC_FFN], a], axis=0)
        tail_ref[:, c0:c0 + FC_FFN] = a[tm - halo:tm]
        a1 = ext[halo - 1:halo - 1 + tm]
        a2 = ext[halo - 2:halo - 2 + tm]
        cv = (cw_ref[2:3, c0:c0 + FC_FFN] * a + cw_ref[1:2, c0:c0 + FC_FFN] * a1
              + cw_ref[0:1, c0:c0 + FC_FFN] * a2 + cb_ref[:, c0:c0 + FC_FFN])
        hmid_ref[:, c0:c0 + FC_FFN] = (cv * _sigmoid(cv) * g).astype(BF16)
    ffn = jnp.dot(hmid_ref[...], wdn_ref[...], preferred_element_type=F32)
    o_ref[0] = _layer_norm(alpha * x1 + ffn, ln2g_ref[...], ln2b_ref[...])


def _mixer_ffn(alpha, x, ys, yr, glu_w, glu_b, s5_gain, w_out, ln1_g, ln1_b,
               w_up, conv_w, conv_b, w_down, ln2_g, ln2_b):
    bsz, seq, d = x.shape
    tm = TM_FFN
    s5w, ret_w = ys.shape[-1], yr.shape[-1]
    dff = w_down.shape[0]
    row = lambda a: a.reshape(1, -1).astype(F32)
    tok = lambda w: pl.BlockSpec((1, tm, w), lambda b, i: (b, i, 0))
    operands = [x, ys, yr, glu_w.astype(BF16), row(glu_b), row(s5_gain), w_out.astype(BF16),
                row(ln1_g), row(ln1_b), w_up.astype(BF16), conv_w.astype(F32), row(conv_b),
                w_down.astype(BF16), row(ln2_g), row(ln2_b)]
    in_specs = [tok(d), tok(s5w), tok(ret_w)] + [_const_spec(a.shape) for a in operands[3:]]
    return pl.pallas_call(
        functools.partial(_ffn_kernel, alpha),
        grid=(bsz, seq // tm),
        in_specs=in_specs,
        out_specs=tok(d),
        out_shape=jax.ShapeDtypeStruct((bsz, seq, d), x.dtype),
        scratch_shapes=[pltpu.VMEM((tm, dff), BF16), pltpu.VMEM((8, dff), F32)],
        compiler_params=pltpu.CompilerParams(
            dimension_semantics=("arbitrary", "arbitrary"), vmem_limit_bytes=VMEM_LIMIT_BYTES),
        name="mixer_ffn",
    )(*operands)


def kernel(x, w_in, s5_lambda_re, s5_lambda_im, s5_b_re, s5_b_im, s5_c_re, s5_c_im, s5_d, s5_log_step,
           s5_glu_w, s5_glu_b, s5_out_gain, ret_gn_gain, w_out, ln1_g, ln1_b, ffn_w_up, ffn_conv_w,
           ffn_conv_b, ffn_w_down, ln2_g, ln2_b):
    depth = w_in.shape[0]
    alpha = (2.0 * depth) ** 0.25
    s5w = s5_d.shape[-1]
    for l in range(depth):
        u, yr = _mixer_front(x, w_in[l], ret_gn_gain[l], s5w)
        mats = _s5_matrices(s5_lambda_re[l], s5_lambda_im[l], s5_b_re[l], s5_b_im[l], s5_c_re[l],
                            s5_c_im[l], s5_d[l], s5_log_step[l], S5_Q)
        ys = _s5_scan(u, mats)
        x = _mixer_ffn(alpha, x, ys, yr, s5_glu_w[l], s5_glu_b[l], s5_out_gain[l], w_out[l], ln1_g[l],
                       ln1_b[l], ffn_w_up[l], ffn_conv_w[l], ffn_conv_b[l], ffn_w_down[l], ln2_g[l], ln2_b[l])
    return x
```

```python
import functools
import math

import numpy as np
import jax
import jax.numpy as jnp
from jax import lax
from jax.experimental import pallas as pl
from jax.experimental.pallas import tpu as pltpu

CHUNK = 64
S5_GROUP = 16
S5_STATE = 64
RET_HEADS = 4
RET_DK = 64
RET_DV = 128
CONV_W = 3
ROPE_BASE = 10000.0
LN_EPS = 1e-5

LANES = 128
SUBLANES_BF16 = 16
VMEM_LIMIT_BYTES = 56 * 1024 * 1024

TL_FRONT = 512
TB_RET = 256
S5_Q = 8
S5_NB = 32
TM_FFN = 512
FC_FFN = 256

F32 = jnp.float32
BF16 = jnp.bfloat16
HI = lax.Precision.HIGHEST


def _const_spec(shape):
    nd = len(shape)
    return pl.BlockSpec(shape, lambda *_: (0,) * nd, pipeline_mode=pl.Buffered(1))


def _layer_norm(v, g, b):
    mu = jnp.mean(v, axis=-1, keepdims=True)
    vc = v - mu
    var = jnp.mean(vc * vc, axis=-1, keepdims=True)
    return vc * lax.rsqrt(var + LN_EPS) * g + b


def _sigmoid(v):
    return 1.0 / (1.0 + jnp.exp(-v))


def _front_kernel(x_ref, w_ref, cos_ref, sin_ref, hm_ref, dmask_ref, xi_ref, zk_ref,
                  gdec_ref, gn_ref, u_ref, yr_ref, state_ref):
    @pl.when(pl.program_id(1) == 0)
    def _():
        state_ref[...] = jnp.zeros_like(state_ref)

    s5w = u_ref.shape[-1]
    xb = x_ref[0].astype(BF16)
    u = jnp.dot(xb, w_ref[:, 0:s5w], preferred_element_type=F32)
    u_ref[0] = u.astype(BF16)

    o_qk = s5w
    qk = jnp.dot(xb, w_ref[:, o_qk:o_qk + 4 * LANES], preferred_element_type=F32)
    cos = cos_ref[...]
    sin = sin_ref[...]
    q1, q2 = qk[:, 0:LANES], qk[:, LANES:2 * LANES]
    k1, k2 = qk[:, 2 * LANES:3 * LANES], qk[:, 3 * LANES:4 * LANES]
    qf = jnp.concatenate([q1 * cos - q2 * sin, q1 * sin + q2 * cos], axis=1)
    kf = jnp.concatenate([k1 * cos - k2 * sin, k1 * sin + k2 * cos], axis=1)

    o_v = o_qk + 4 * LANES
    ret_w = RET_HEADS * RET_DV
    v = jnp.dot(xb, w_ref[:, o_v:o_v + ret_w], preferred_element_type=F32).astype(BF16)
    gate = jnp.dot(xb, w_ref[:, o_v + ret_w:o_v + 2 * ret_w], preferred_element_type=F32)

    tl = xb.shape[0]
    for blk in range(tl // TB_RET):
        r0 = blk * TB_RET
        qb = qf[r0:r0 + TB_RET].astype(BF16)
        kb = kf[r0:r0 + TB_RET]
        for hd in range(RET_HEADS):
            c0 = hd * RET_DV
            kh = (kb * hm_ref[hd]).astype(BF16)
            sc = lax.dot_general(qb, kh, (((1,), (1,)), ((), ())), preferred_element_type=F32)
            p = (sc * dmask_ref[hd]).astype(BF16)
            vh = v[r0:r0 + TB_RET, c0:c0 + RET_DV]
            st = state_ref[hd]
            o = jnp.dot(p, vh, preferred_element_type=F32)
            o = o + xi_ref[hd] * jnp.dot(qb, st.astype(BF16), preferred_element_type=F32)
            kz = (kb * zk_ref[hd]).astype(BF16)
            upd = lax.dot_general(kz, vh, (((0,), (0,)), ((), ())), preferred_element_type=F32)
            state_ref[hd] = st * gdec_ref[hd] + upd
            mu = jnp.mean(o, axis=-1, keepdims=True)
            oc = o - mu
            var = jnp.mean(oc * oc, axis=-1, keepdims=True)
            on = oc * lax.rsqrt(var + LN_EPS) * gn_ref[:, c0:c0 + RET_DV]
            g = gate[r0:r0 + TB_RET, c0:c0 + RET_DV]
            yr_ref[0, r0:r0 + TB_RET, c0:c0 + RET_DV] = (on * (g * _sigmoid(g))).astype(BF16)


def _retention_consts(tb):
    log_gamma = jnp.log1p(-(2.0 ** (-5.0 - jnp.arange(RET_HEADS, dtype=F32))))
    pos = jnp.arange(tb, dtype=F32)
    chunk = jnp.arange(tb) // CHUNK
    visible = (chunk[None, :] <= chunk[:, None]).astype(F32)
    dist = jnp.abs(pos[:, None] - pos[None, :])
    dmask = jnp.exp(dist[None] * log_gamma[:, None, None]) * visible[None]
    xi = jnp.exp((pos + 1.0)[None, :] * log_gamma[:, None])
    zeta = jnp.exp((tb - 1.0 - pos)[None, :] * log_gamma[:, None])
    gdec = jnp.exp(tb * log_gamma)
    lane = np.arange(2 * LANES) % LANES
    half = RET_DK // 2
    hm = np.stack([(lane // half == h) for h in range(RET_HEADS)]).astype(np.float32)
    hm = jnp.asarray(hm) * (RET_DK ** -0.5)
    xi_b = jnp.broadcast_to(xi[:, :, None], (RET_HEADS, tb, RET_DV))
    zk = zeta[:, :, None] * hm[:, None, :]
    gdec_b = jnp.broadcast_to(gdec[:, None, None], (RET_HEADS, 2 * LANES, RET_DV))
    return hm[:, None, :], dmask, xi_b, zk, gdec_b


def _front_weight(w_in, s5w):
    half = RET_DK // 2
    first = np.concatenate([np.arange(h * RET_DK, h * RET_DK + half) for h in range(RET_HEADS)])
    second = first + half
    nqk = RET_HEADS * RET_DK
    perm = np.concatenate([np.arange(s5w), s5w + first, s5w + second, s5w + nqk + first,
                           s5w + nqk + second, np.arange(s5w + 2 * nqk, w_in.shape[1])])
    return w_in[:, perm].astype(BF16)


def _mixer_front(x, w_in, gn_gain, s5w):
    bsz, seq, d = x.shape
    tl = TL_FRONT
    ret_w = RET_HEADS * RET_DV
    half = RET_DK // 2
    freqs = ROPE_BASE ** (-jnp.arange(half, dtype=F32) / half)
    ang = jnp.arange(seq, dtype=F32)[:, None] * freqs[None, :]
    cos = jnp.tile(jnp.cos(ang), (1, RET_HEADS))
    sin = jnp.tile(jnp.sin(ang), (1, RET_HEADS))
    hm, dmask, xi_b, zk, gdec_b = _retention_consts(TB_RET)
    w = _front_weight(w_in, s5w)
    return pl.pallas_call(
        _front_kernel,
        grid=(bsz, seq // tl),
        in_specs=[
            pl.BlockSpec((1, tl, d), lambda b, i: (b, i, 0)),
            _const_spec(w.shape),
            pl.BlockSpec((tl, LANES), lambda b, i: (i, 0)),
            pl.BlockSpec((tl, LANES), lambda b, i: (i, 0)),
            _const_spec(hm.shape), _const_spec(dmask.shape), _const_spec(xi_b.shape),
            _const_spec(zk.shape), _const_spec(gdec_b.shape), _const_spec((1, ret_w)),
        ],
        out_specs=[
            pl.BlockSpec((1, tl, s5w), lambda b, i: (b, i, 0)),
            pl.BlockSpec((1, tl, ret_w), lambda b, i: (b, i, 0)),
        ],
        out_shape=[jax.ShapeDtypeStruct((bsz, seq, s5w), BF16),
                   jax.ShapeDtypeStruct((bsz, seq, ret_w), BF16)],
        scratch_shapes=[pltpu.VMEM((RET_HEADS, 2 * LANES, RET_DV), F32)],
        compiler_params=pltpu.CompilerParams(
            dimension_semantics=("arbitrary", "arbitrary"), vmem_limit_bytes=VMEM_LIMIT_BYTES),
        name="mixer_front",
    )(x, w, cos, sin, hm, dmask, xi_b, zk, gdec_b, gn_gain.reshape(1, ret_w).astype(F32))


def _same_group(shape, row_shift, col_shift):
    gpl_mask = LANES // S5_GROUP - 1
    rows = lax.broadcasted_iota(jnp.int32, shape, 0) >> row_shift
    cols = (lax.broadcasted_iota(jnp.int32, shape, 1) >> col_shift) & gpl_mask
    return (rows == cols).astype(F32)


def _s5_build(kt_ref, at_ref, ct_ref, t_ref, e_ref, c_ref):
    q = kt_ref.shape[1]
    gpl = LANES // S5_GROUP
    ch_shift = S5_GROUP.bit_length() - 1
    st_shift = S5_STATE.bit_length() - 1
    t_ref[...] = jnp.zeros_like(t_ref)
    mask_t = _same_group((LANES, LANES), ch_shift, ch_shift)
    for lag in range(q):
        bd = (jnp.concatenate([kt_ref[0, lag]] * gpl, axis=0) * mask_t).astype(BF16)
        for i in range(q - lag):
            j = i + lag
            t_ref[i * LANES:(i + 1) * LANES, j * LANES:(j + 1) * LANES] = bd
    mask_e = _same_group((LANES, e_ref.shape[1]), ch_shift, st_shift)
    for i in range(q):
        e_ref[i * LANES:(i + 1) * LANES, :] = (jnp.concatenate([at_ref[0, i]] * gpl, axis=0) * mask_e).astype(BF16)
    ns = c_ref.shape[0] // 2
    mask_c = _same_group((ns, c_ref.shape[1]), st_shift, ch_shift)
    for part in range(2):
        c_ref[part * ns:(part + 1) * ns, :] = (jnp.concatenate([ct_ref[0, part]] * gpl, axis=0) * mask_c).astype(BF16)


def _s5_kernel(u_ref, kt_ref, at_ref, ct_ref, aq_ref, y_ref, t_ref, e_ref, c_ref, sprev_ref, carry_ref):
    @pl.when(pl.program_id(1) == 0)
    def _():
        carry_ref[...] = jnp.zeros_like(carry_ref)
        _s5_build(kt_ref, at_ref, ct_ref, t_ref, e_ref, c_ref)

    nb, q, bsz, _ = u_ref.shape
    xcat = jnp.concatenate([u_ref[:, j].reshape(nb * bsz, LANES) for j in range(q)], axis=1)
    e = jnp.dot(xcat, e_ref[...], preferred_element_type=F32)
    ns = e.shape[1] // 2
    aqr = aq_ref[0, :, 0:ns]
    aqi = aq_ref[0, :, ns:2 * ns]
    s_re = carry_ref[:, 0:ns]
    s_im = carry_ref[:, ns:2 * ns]
    for n in range(nb):
        r0 = n * bsz
        sprev_ref[r0:r0 + bsz, :] = jnp.concatenate([s_re, s_im], axis=1).astype(BF16)
        e_re = e[r0:r0 + bsz, 0:ns]
        e_im = e[r0:r0 + bsz, ns:2 * ns]
        s_re, s_im = aqr * s_re - aqi * s_im + e_re, aqr * s_im + aqi * s_re + e_im
    carry_ref[...] = jnp.concatenate([s_re, s_im], axis=1)
    y = jnp.dot(xcat, t_ref[...], preferred_element_type=F32)
    y = y + jnp.dot(sprev_ref[...], c_ref[...], preferred_element_type=F32)
    for j in range(q):
        y_ref[:, j] = y[:, j * LANES:(j + 1) * LANES].reshape(nb, bsz, LANES).astype(BF16)


def _s5_matrices(lam_re, lam_im, b_re, b_im, c_re, c_im, d, log_step, q):
    g, p = lam_re.shape
    cch = b_re.shape[-1]
    gpl = LANES // cch
    nblk = g // gpl
    dt = jnp.exp(log_step)[:, None]
    zr, zi = lam_re * dt, lam_im * dt
    ks = jnp.arange(q + 1, dtype=F32)[:, None, None]
    mag = jnp.exp(ks * zr[None])
    pr, pi = mag * jnp.cos(ks * zi[None]), mag * jnp.sin(ks * zi[None])
    nr, ni = pr[1] - 1.0, pi[1]
    den = lam_re * lam_re + lam_im * lam_im
    fr = (nr * lam_re + ni * lam_im) / den
    fi = (ni * lam_re - nr * lam_im) / den
    bbr = fr[..., None] * b_re - fi[..., None] * b_im
    bbi = fr[..., None] * b_im + fi[..., None] * b_re
    wr = pr[:q, :, :, None] * bbr[None] - pi[:q, :, :, None] * bbi[None]
    wi = pr[:q, :, :, None] * bbi[None] + pi[:q, :, :, None] * bbr[None]
    kern = (jnp.einsum('gop,kgpc->kgoc', c_re, wr, precision=HI)
            - jnp.einsum('gop,kgpc->kgoc', c_im, wi, precision=HI))
    kern = kern.at[0].add(jnp.eye(cch, dtype=F32)[None] * d.reshape(g, 1, cch))
    kt = kern.reshape(q, nblk, gpl, cch, cch).transpose(1, 0, 4, 2, 3).reshape(nblk, q, cch, LANES)
    er = wr[::-1].reshape(q, nblk, gpl, p, cch).transpose(1, 0, 4, 2, 3)
    ei = wi[::-1].reshape(q, nblk, gpl, p, cch).transpose(1, 0, 4, 2, 3)
    at = jnp.concatenate([er.reshape(nblk, q, cch, gpl * p), ei.reshape(nblk, q, cch, gpl * p)], axis=3)
    cr = c_re[None] * pr[1:, :, None, :] - c_im[None] * pi[1:, :, None, :]
    ci = c_re[None] * pi[1:, :, None, :] + c_im[None] * pr[1:, :, None, :]
    cr = cr.reshape(q, nblk, gpl, cch, p).transpose(1, 4, 0, 2, 3).reshape(nblk, p, q * LANES)
    ci = ci.reshape(q, nblk, gpl, cch, p).transpose(1, 4, 0, 2, 3).reshape(nblk, p, q * LANES)
    ct = jnp.stack([cr, -ci], axis=1)
    aq = jnp.concatenate([pr[q].reshape(nblk, gpl * p), pi[q].reshape(nblk, gpl * p)], axis=1)
    return kt, at, ct, aq


def _s5_scan(u, mats):
    kt, at, ct, aq = mats
    bsz, seq, w = u.shape
    q, nb = S5_Q, S5_NB
    nblk = w // LANES
    ns2 = aq.shape[-1]
    u4 = jnp.transpose(u, (1, 0, 2)).reshape(seq // q, q, bsz, w)
    aq_b = jnp.broadcast_to(aq[:, None, :], (nblk, bsz, ns2))
    blk = pl.BlockSpec((nb, q, bsz, LANES), lambda m, i: (i, 0, 0, m))
    tab = lambda a: pl.BlockSpec((1,) + a.shape[1:], lambda m, i: (m,) + (0,) * (a.ndim - 1))
    y4 = pl.pallas_call(
        _s5_kernel,
        grid=(nblk, seq // (q * nb)),
        in_specs=[blk, tab(kt), tab(at), tab(ct), tab(aq_b)],
        out_specs=blk,
        out_shape=jax.ShapeDtypeStruct(u4.shape, BF16),
        scratch_shapes=[pltpu.VMEM((q * LANES, q * LANES), BF16),
                        pltpu.VMEM((q * LANES, ns2), BF16),
                        pltpu.VMEM((ns2, q * LANES), BF16),
                        pltpu.VMEM((nb * bsz, ns2), BF16),
                        pltpu.VMEM((bsz, ns2), F32)],
        compiler_params=pltpu.CompilerParams(
            dimension_semantics=("arbitrary", "arbitrary"), vmem_limit_bytes=VMEM_LIMIT_BYTES),
        name="s5_scan",
    )(u4, kt, at, ct, aq_b)
    return jnp.transpose(y4.reshape(seq, bsz, w), (1, 0, 2))


def _ffn_kernel(alpha, x_ref, ys_ref, yr_ref, gluw_ref, glub_ref, s5g_ref, wout_ref, ln1g_ref, ln1b_ref,
                wup_ref, cw_ref, cb_ref, wdn_ref, ln2g_ref, ln2b_ref, o_ref, hmid_ref, tail_ref):
    @pl.when(pl.program_id(1) == 0)
    def _():
        tail_ref[...] = jnp.zeros_like(tail_ref)

    x = x_ref[0]
    tm = x.shape[0]
    s5w = ys_ref.shape[-1]
    y = ys_ref[0].astype(F32)
    cdf = 0.5 * (1.0 + jnp.tanh(math.sqrt(2.0 / math.pi) * (y + 0.044715 * (y * y * y))))
    y = y * cdf
    z = jnp.dot(y.astype(BF16), gluw_ref[...], preferred_element_type=F32) + glub_ref[...]
    y = y * _sigmoid(z)
    y = y * lax.rsqrt(jnp.mean(y * y, axis=-1, keepdims=True) + LN_EPS) * s5g_ref[...]
    mix = jnp.dot(y.astype(BF16), wout_ref[0:s5w, :], preferred_element_type=F32)
    mix = mix + jnp.dot(yr_ref[0], wout_ref[s5w:, :], preferred_element_type=F32)
    x1 = _layer_norm(alpha * x + mix, ln1g_ref[...], ln1b_ref[...])

    xb = x1.astype(BF16)
    dff = hmid_ref.shape[1]
    halo = tail_ref.shape[0]
    for c in range(dff // FC_FFN):
        c0 = c * FC_FFN
        a = jnp.dot(xb, wup_ref[:, c0:c0 + FC_FFN], preferred_element_type=F32)
        g = jnp.dot(xb, wup_ref[:, dff + c0:dff + c0 + FC_FFN], preferred_element_type=F32)
        ext = jnp.concatenate([tail_ref[:, c0:c0 + FC_FFN], a], axis=0)
        tail_ref[:, c0:c0 + FC_FFN] = a[tm - halo:tm]
        a1 = ext[halo - 1:halo - 1 + tm]
        a2 = ext[halo - 2:halo - 2 + tm]
        cv = (cw_ref[2:3, c0:c0 + FC_FFN] * a + cw_ref[1:2, c0:c0 + FC_FFN] * a1
              + cw_ref[0:1, c0:c0 + FC_FFN] * a2 + cb_ref[:, c0:c0 + FC_FFN])
        hmid_ref[:, c0:c0 + FC_FFN] = (cv * _sigmoid(cv) * g).astype(BF16)
    ffn = jnp.dot(hmid_ref[...], wdn_ref[...], preferred_element_type=F32)
    o_ref[0] = _layer_norm(alpha * x1 + ffn, ln2g_ref[...], ln2b_ref[...])


def _mixer_ffn(alpha, x, ys, yr, glu_w, glu_b, s5_gain, w_out, ln1_g, ln1_b,
               w_up, conv_w, conv_b, w_down, ln2_g, ln2_b):
    bsz, seq, d = x.shape
    tm = TM_FFN
    s5w, ret_w = ys.shape[-1], yr.shape[-1]
    dff = w_down.shape[0]
    row = lambda a: a.reshape(1, -1).astype(F32)
    tok = lambda w: pl.BlockSpec((1, tm, w), lambda b, i: (b, i, 0))
    operands = [x, ys, yr, glu_w.astype(BF16), row(glu_b), row(s5_gain), w_out.astype(BF16),
                row(ln1_g), row(ln1_b), w_up.astype(BF16), conv_w.astype(F32), row(conv_b),
                w_down.astype(BF16), row(ln2_g), row(ln2_b)]
    in_specs = [tok(d), tok(s5w), tok(ret_w)] + [_const_spec(a.shape) for a in operands[3:]]
    return pl.pallas_call(
        functools.partial(_ffn_kernel, alpha),
        grid=(bsz, seq // tm),
        in_specs=in_specs,
        out_specs=tok(d),
        out_shape=jax.ShapeDtypeStruct((bsz, seq, d), x.dtype),
        scratch_shapes=[pltpu.VMEM((tm, dff), BF16), pltpu.VMEM((8, dff), F32)],
        compiler_params=pltpu.CompilerParams(
            dimension_semantics=("arbitrary", "arbitrary"), vmem_limit_bytes=VMEM_LIMIT_BYTES),
        name="mixer_ffn",
    )(*operands)


def kernel(x, w_in, s5_lambda_re, s5_lambda_im, s5_b_re, s5_b_im, s5_c_re, s5_c_im, s5_d, s5_log_step,
           s5_glu_w, s5_glu_b, s5_out_gain, ret_gn_gain, w_out, ln1_g, ln1_b, ffn_w_up, ffn_conv_w,
           ffn_conv_b, ffn_w_down, ln2_g, ln2_b):
    depth = w_in.shape[0]
    alpha = (2.0 * depth) ** 0.25
    s5w = s5_d.shape[-1]
    for l in range(depth):
        u, yr = _mixer_front(x, w_in[l], ret_gn_gain[l], s5w)
        mats = _s5_matrices(s5_lambda_re[l], s5_lambda_im[l], s5_b_re[l], s5_b_im[l], s5_c_re[l],
                            s5_c_im[l], s5_d[l], s5_log_step[l], S5_Q)
        ys = _s5_scan(u, mats)
        x = _mixer_ffn(alpha, x, ys, yr, s5_glu_w[l], s5_glu_b[l], s5_out_gain[l], w_out[l], ln1_g[l],
                       ln1_b[l], ffn_w_up[l], ffn_conv_w[l], ffn_conv_b[l], ffn_w_down[l], ln2_g[l], ln2_b[l])
    return x
```

```python
import functools
import math

import numpy as np
import jax
import jax.numpy as jnp
from jax import lax
from jax.experimental import pallas as pl
from jax.experimental.pallas import tpu as pltpu

CHUNK = 64
S5_GROUP = 16
S5_STATE = 64
RET_HEADS = 4
RET_DK = 64
RET_DV = 128
CONV_W = 3
ROPE_BASE = 10000.0
LN_EPS = 1e-5

LANES = 128
SUBLANES_BF16 = 16
VMEM_LIMIT_BYTES = 56 * 1024 * 1024

TL_FRONT = 512
TB_RET = 256
S5_Q = 8
S5_NB = 32
TM_FFN = 256
FC_FFN = 256
FFN_PIECE = 256

F32 = jnp.float32
BF16 = jnp.bfloat16
HI = lax.Precision.HIGHEST


def _const_spec(shape):
    nd = len(shape)
    return pl.BlockSpec(shape, lambda *_: (0,) * nd, pipeline_mode=pl.Buffered(1))


def _layer_norm(v, g, b):
    mu = jnp.mean(v, axis=-1, keepdims=True)
    vc = v - mu
    var = jnp.mean(vc * vc, axis=-1, keepdims=True)
    return vc * lax.rsqrt(var + LN_EPS) * g + b


def _sigmoid(v):
    return 1.0 / (1.0 + jnp.exp(-v))


def _front_kernel(x_ref, w_ref, cos_ref, sin_ref, hm_ref, dmask_ref, xi_ref, zk_ref,
                  gdec_ref, gn_ref, u_ref, yr_ref, state_ref):
    @pl.when(pl.program_id(1) == 0)
    def _():
        state_ref[...] = jnp.zeros_like(state_ref)

    s5w = u_ref.shape[-1]
    xb = x_ref[0].astype(BF16)
    u = jnp.dot(xb, w_ref[:, 0:s5w], preferred_element_type=F32)
    u_ref[0] = u.astype(BF16)

    o_qk = s5w
    qk = jnp.dot(xb, w_ref[:, o_qk:o_qk + 4 * LANES], preferred_element_type=F32)
    cos = cos_ref[...]
    sin = sin_ref[...]
    q1, q2 = qk[:, 0:LANES], qk[:, LANES:2 * LANES]
    k1, k2 = qk[:, 2 * LANES:3 * LANES], qk[:, 3 * LANES:4 * LANES]
    qf = jnp.concatenate([q1 * cos - q2 * sin, q1 * sin + q2 * cos], axis=1)
    kf = jnp.concatenate([k1 * cos - k2 * sin, k1 * sin + k2 * cos], axis=1)

    o_v = o_qk + 4 * LANES
    ret_w = RET_HEADS * RET_DV
    v = jnp.dot(xb, w_ref[:, o_v:o_v + ret_w], preferred_element_type=F32).astype(BF16)
    gate = jnp.dot(xb, w_ref[:, o_v + ret_w:o_v + 2 * ret_w], preferred_element_type=F32)

    tl = xb.shape[0]
    for blk in range(tl // TB_RET):
        r0 = blk * TB_RET
        qb = qf[r0:r0 + TB_RET].astype(BF16)
        kb = kf[r0:r0 + TB_RET]
        for hd in range(RET_HEADS):
            c0 = hd * RET_DV
            kh = (kb * hm_ref[hd]).astype(BF16)
            sc = lax.dot_general(qb, kh, (((1,), (1,)), ((), ())), preferred_element_type=F32)
            p = (sc * dmask_ref[hd]).astype(BF16)
            vh = v[r0:r0 + TB_RET, c0:c0 + RET_DV]
            st = state_ref[hd]
            o = jnp.dot(p, vh, preferred_element_type=F32)
            o = o + xi_ref[hd] * jnp.dot(qb, st.astype(BF16), preferred_element_type=F32)
            kz = (kb * zk_ref[hd]).astype(BF16)
            upd = lax.dot_general(kz, vh, (((0,), (0,)), ((), ())), preferred_element_type=F32)
            state_ref[hd] = st * gdec_ref[hd] + upd
            mu = jnp.mean(o, axis=-1, keepdims=True)
            oc = o - mu
            var = jnp.mean(oc * oc, axis=-1, keepdims=True)
            on = oc * lax.rsqrt(var + LN_EPS) * gn_ref[:, c0:c0 + RET_DV]
            g = gate[r0:r0 + TB_RET, c0:c0 + RET_DV]
            yr_ref[0, r0:r0 + TB_RET, c0:c0 + RET_DV] = (on * (g * _sigmoid(g))).astype(BF16)


def _retention_consts(tb):
    log_gamma = jnp.log1p(-(2.0 ** (-5.0 - jnp.arange(RET_HEADS, dtype=F32))))
    pos = jnp.arange(tb, dtype=F32)
    chunk = jnp.arange(tb) // CHUNK
    visible = (chunk[None, :] <= chunk[:, None]).astype(F32)
    dist = jnp.abs(pos[:, None] - pos[None, :])
    dmask = jnp.exp(dist[None] * log_gamma[:, None, None]) * visible[None]
    xi = jnp.exp((pos + 1.0)[None, :] * log_gamma[:, None])
    zeta = jnp.exp((tb - 1.0 - pos)[None, :] * log_gamma[:, None])
    gdec = jnp.exp(tb * log_gamma)
    lane = np.arange(2 * LANES) % LANES
    half = RET_DK // 2
    hm = np.stack([(lane // half == h) for h in range(RET_HEADS)]).astype(np.float32)
    hm = jnp.asarray(hm) * (RET_DK ** -0.5)
    xi_b = jnp.broadcast_to(xi[:, :, None], (RET_HEADS, tb, RET_DV))
    zk = zeta[:, :, None] * hm[:, None, :]
    gdec_b = jnp.broadcast_to(gdec[:, None, None], (RET_HEADS, 2 * LANES, RET_DV))
    return hm[:, None, :], dmask, xi_b, zk, gdec_b


def _front_weight(w_in, s5w):
    half = RET_DK // 2
    first = np.concatenate([np.arange(h * RET_DK, h * RET_DK + half) for h in range(RET_HEADS)])
    second = first + half
    nqk = RET_HEADS * RET_DK
    perm = np.concatenate([np.arange(s5w), s5w + first, s5w + second, s5w + nqk + first,
                           s5w + nqk + second, np.arange(s5w + 2 * nqk, w_in.shape[1])])
    return w_in[:, perm].astype(BF16)


def _mixer_front(x, w_in, gn_gain, s5w):
    bsz, seq, d = x.shape
    tl = TL_FRONT
    ret_w = RET_HEADS * RET_DV
    half = RET_DK // 2
    freqs = ROPE_BASE ** (-jnp.arange(half, dtype=F32) / half)
    ang = jnp.arange(seq, dtype=F32)[:, None] * freqs[None, :]
    cos = jnp.tile(jnp.cos(ang), (1, RET_HEADS))
    sin = jnp.tile(jnp.sin(ang), (1, RET_HEADS))
    hm, dmask, xi_b, zk, gdec_b = _retention_consts(TB_RET)
    w = _front_weight(w_in, s5w)
    return pl.pallas_call(
        _front_kernel,
        grid=(bsz, seq // tl),
        in_specs=[
            pl.BlockSpec((1, tl, d), lambda b, i: (b, i, 0)),
            _const_spec(w.shape),
            pl.BlockSpec((tl, LANES), lambda b, i: (i, 0)),
            pl.BlockSpec((tl, LANES), lambda b, i: (i, 0)),
            _const_spec(hm.shape), _const_spec(dmask.shape), _const_spec(xi_b.shape),
            _const_spec(zk.shape), _const_spec(gdec_b.shape), _const_spec((1, ret_w)),
        ],
        out_specs=[
            pl.BlockSpec((1, tl, s5w), lambda b, i: (b, i, 0)),
            pl.BlockSpec((1, tl, ret_w), lambda b, i: (b, i, 0)),
        ],
        out_shape=[jax.ShapeDtypeStruct((bsz, seq, s5w), BF16),
                   jax.ShapeDtypeStruct((bsz, seq, ret_w), BF16)],
        scratch_shapes=[pltpu.VMEM((RET_HEADS, 2 * LANES, RET_DV), F32)],
        compiler_params=pltpu.CompilerParams(
            dimension_semantics=("arbitrary", "arbitrary"), vmem_limit_bytes=VMEM_LIMIT_BYTES),
        name="mixer_front",
    )(x, w, cos, sin, hm, dmask, xi_b, zk, gdec_b, gn_gain.reshape(1, ret_w).astype(F32))


def _same_group(shape, row_shift, col_shift):
    gpl_mask = LANES // S5_GROUP - 1
    rows = lax.broadcasted_iota(jnp.int32, shape, 0) >> row_shift
    cols = (lax.broadcasted_iota(jnp.int32, shape, 1) >> col_shift) & gpl_mask
    return (rows == cols).astype(F32)


def _s5_build(kt_ref, at_ref, ct_ref, t_ref, e_ref, c_ref):
    q = kt_ref.shape[1]
    gpl = LANES // S5_GROUP
    ch_shift = S5_GROUP.bit_length() - 1
    st_shift = S5_STATE.bit_length() - 1
    t_ref[...] = jnp.zeros_like(t_ref)
    mask_t = _same_group((LANES, LANES), ch_shift, ch_shift)
    for lag in range(q):
        bd = (jnp.concatenate([kt_ref[0, lag]] * gpl, axis=0) * mask_t).astype(BF16)
        for i in range(q - lag):
            j = i + lag
            t_ref[i * LANES:(i + 1) * LANES, j * LANES:(j + 1) * LANES] = bd
    mask_e = _same_group((LANES, e_ref.shape[1]), ch_shift, st_shift)
    for i in range(q):
        e_ref[i * LANES:(i + 1) * LANES, :] = (jnp.concatenate([at_ref[0, i]] * gpl, axis=0) * mask_e).astype(BF16)
    ns = c_ref.shape[0] // 2
    mask_c = _same_group((ns, c_ref.shape[1]), st_shift, ch_shift)
    for part in range(2):
        c_ref[part * ns:(part + 1) * ns, :] = (jnp.concatenate([ct_ref[0, part]] * gpl, axis=0) * mask_c).astype(BF16)


def _s5_kernel(u_ref, kt_ref, at_ref, ct_ref, aq_ref, y_ref, t_ref, e_ref, c_ref, sprev_ref, carry_ref):
    @pl.when(pl.program_id(1) == 0)
    def _():
        carry_ref[...] = jnp.zeros_like(carry_ref)
        _s5_build(kt_ref, at_ref, ct_ref, t_ref, e_ref, c_ref)

    nb, q, bsz, _ = u_ref.shape
    xcat = jnp.concatenate([u_ref[:, j].reshape(nb * bsz, LANES) for j in range(q)], axis=1)
    e = jnp.dot(xcat, e_ref[...], preferred_element_type=F32)
    ns = e.shape[1] // 2
    aqr = aq_ref[0, :, 0:ns]
    aqi = aq_ref[0, :, ns:2 * ns]
    s_re = carry_ref[:, 0:ns]
    s_im = carry_ref[:, ns:2 * ns]
    for n in range(nb):
        r0 = n * bsz
        sprev_ref[r0:r0 + bsz, :] = jnp.concatenate([s_re, s_im], axis=1).astype(BF16)
        e_re = e[r0:r0 + bsz, 0:ns]
        e_im = e[r0:r0 + bsz, ns:2 * ns]
        s_re, s_im = aqr * s_re - aqi * s_im + e_re, aqr * s_im + aqi * s_re + e_im
    carry_ref[...] = jnp.concatenate([s_re, s_im], axis=1)
    y = jnp.dot(xcat, t_ref[...], preferred_element_type=F32)
    y = y + jnp.dot(sprev_ref[...], c_ref[...], preferred_element_type=F32)
    for j in range(q):
        y_ref[:, j] = y[:, j * LANES:(j + 1) * LANES].reshape(nb, bsz, LANES).astype(BF16)


def _s5_matrices(lam_re, lam_im, b_re, b_im, c_re, c_im, d, log_step, q):
    g, p = lam_re.shape
    cch = b_re.shape[-1]
    gpl = LANES // cch
    nblk = g // gpl
    dt = jnp.exp(log_step)[:, None]
    zr, zi = lam_re * dt, lam_im * dt
    ks = jnp.arange(q + 1, dtype=F32)[:, None, None]
    mag = jnp.exp(ks * zr[None])
    pr, pi = mag * jnp.cos(ks * zi[None]), mag * jnp.sin(ks * zi[None])
    nr, ni = pr[1] - 1.0, pi[1]
    den = lam_re * lam_re + lam_im * lam_im
    fr = (nr * lam_re + ni * lam_im) / den
    fi = (ni * lam_re - nr * lam_im) / den
    bbr = fr[..., None] * b_re - fi[..., None] * b_im
    bbi = fr[..., None] * b_im + fi[..., None] * b_re
    wr = pr[:q, :, :, None] * bbr[None] - pi[:q, :, :, None] * bbi[None]
    wi = pr[:q, :, :, None] * bbi[None] + pi[:q, :, :, None] * bbr[None]
    kern = (jnp.einsum('gop,kgpc->kgoc', c_re, wr, precision=HI)
            - jnp.einsum('gop,kgpc->kgoc', c_im, wi, precision=HI))
    kern = kern.at[0].add(jnp.eye(cch, dtype=F32)[None] * d.reshape(g, 1, cch))
    kt = kern.reshape(q, nblk, gpl, cch, cch).transpose(1, 0, 4, 2, 3).reshape(nblk, q, cch, LANES)
    er = wr[::-1].reshape(q, nblk, gpl, p, cch).transpose(1, 0, 4, 2, 3)
    ei = wi[::-1].reshape(q, nblk, gpl, p, cch).transpose(1, 0, 4, 2, 3)
    at = jnp.concatenate([er.reshape(nblk, q, cch, gpl * p), ei.reshape(nblk, q, cch, gpl * p)], axis=3)
    cr = c_re[None] * pr[1:, :, None, :] - c_im[None] * pi[1:, :, None, :]
    ci = c_re[None] * pi[1:, :, None, :] + c_im[None] * pr[1:, :, None, :]
    cr = cr.reshape(q, nblk, gpl, cch, p).transpose(1, 4, 0, 2, 3).reshape(nblk, p, q * LANES)
    ci = ci.reshape(q, nblk, gpl, cch, p).transpose(1, 4, 0, 2, 3).reshape(nblk, p, q * LANES)
    ct = jnp.stack([cr, -ci], axis=1)
    aq = jnp.concatenate([pr[q].reshape(nblk, gpl * p), pi[q].reshape(nblk, gpl * p)], axis=1)
    return kt, at, ct, aq


def _s5_scan(u, mats):
    kt, at, ct, aq = mats
    bsz, seq, w = u.shape
    q, nb = S5_Q, S5_NB
    nblk = w // LANES
    ns2 = aq.shape[-1]
    u4 = jnp.transpose(u, (1, 0, 2)).reshape(seq // q, q, bsz, w)
    aq_b = jnp.broadcast_to(aq[:, None, :], (nblk, bsz, ns2))
    blk = pl.BlockSpec((nb, q, bsz, LANES), lambda m, i: (i, 0, 0, m))
    tab = lambda a: pl.BlockSpec((1,) + a.shape[1:], lambda m, i: (m,) + (0,) * (a.ndim - 1))
    y4 = pl.pallas_call(
        _s5_kernel,
        grid=(nblk, seq // (q * nb)),
        in_specs=[blk, tab(kt), tab(at), tab(ct), tab(aq_b)],
        out_specs=blk,
        out_shape=jax.ShapeDtypeStruct(u4.shape, BF16),
        scratch_shapes=[pltpu.VMEM((q * LANES, q * LANES), BF16),
                        pltpu.VMEM((q * LANES, ns2), BF16),
                        pltpu.VMEM((ns2, q * LANES), BF16),
                        pltpu.VMEM((nb * bsz, ns2), BF16),
                        pltpu.VMEM((bsz, ns2), F32)],
        compiler_params=pltpu.CompilerParams(
            dimension_semantics=("arbitrary", "arbitrary"), vmem_limit_bytes=VMEM_LIMIT_BYTES),
        name="s5_scan",
    )(u4, kt, at, ct, aq_b)
    return jnp.transpose(y4.reshape(seq, bsz, w), (1, 0, 2))


def _anchor(v, out_width):
    rows, width = v.shape
    folded = jnp.sum(v.reshape(rows // 8, 8, width), axis=0)
    return sum(folded[:, i * out_width:(i + 1) * out_width] for i in range(width // out_width))


def _ffn_kernel(alpha, pairs_per_seq, x_ref, ys_ref, yr_ref, gluw_ref, glub_ref, s5g_ref, wout_ref,
                ln1g_ref, ln1b_ref, wup_ref, cw_ref, cb_ref, wdn_ref, ln2g_ref, ln2b_ref, o_ref,
                x1_ref, xb_ref, res_ref, hmid_ref, tail_ref):
    k = pl.program_id(0)

    @pl.when(k == 0)
    def _():
        x1_ref[...] = jnp.zeros_like(x1_ref)
        xb_ref[...] = jnp.zeros_like(xb_ref)
        res_ref[...] = jnp.zeros_like(res_ref)
        tail_ref[...] = jnp.zeros_like(tail_ref)

    tm = res_ref.shape[1]
    s5w = ys_ref.shape[-1]
    dff = hmid_ref.shape[1]
    halo = tail_ref.shape[0]
    pr = FFN_PIECE
    n_chunk = dff // FC_FFN
    n_slice = 2 * (tm // pr)
    ws = res_ref.shape[-1] // n_slice

    def out_piece(r0, ph):
        o = _layer_norm(res_ref[1 - ph, r0:r0 + pr, :], ln2g_ref[...], ln2b_ref[...])
        o_ref[0, ph * tm + r0:ph * tm + r0 + pr, :] = o
        return _anchor(o, ws)

    def mix_piece(r0, ph):
        rows = slice(ph * tm + r0, ph * tm + r0 + pr)
        y = ys_ref[0, rows, :].astype(F32)
        cdf = 0.5 * (1.0 + jnp.tanh(math.sqrt(2.0 / math.pi) * (y + 0.044715 * (y * y * y))))
        y = y * cdf
        z = jnp.dot(y.astype(BF16), gluw_ref[...], preferred_element_type=F32) + glub_ref[...]
        y = y * _sigmoid(z)
        y = y * lax.rsqrt(jnp.mean(y * y, axis=-1, keepdims=True) + LN_EPS) * s5g_ref[...]
        mix = jnp.dot(y.astype(BF16), wout_ref[0:s5w, :], preferred_element_type=F32)
        mix = mix + jnp.dot(yr_ref[0, rows, :], wout_ref[s5w:, :], preferred_element_type=F32)
        x1 = _layer_norm(alpha * x_ref[0, rows, :] + mix, ln1g_ref[...], ln1b_ref[...])
        x1_ref[ph, r0:r0 + pr, :] = x1
        xb_ref[ph, r0:r0 + pr, :] = x1.astype(BF16)
        return _anchor(x1, ws)

    def mlp_chunk(c, cur, keep):
        c0 = c * FC_FFN
        xb = xb_ref[cur]
        a = jnp.dot(xb, wup_ref[:, c0:c0 + FC_FFN], preferred_element_type=F32)
        g = jnp.dot(xb, wup_ref[:, dff + c0:dff + c0 + FC_FFN], preferred_element_type=F32)
        prev = tail_ref[:, c0:c0 + FC_FFN]
        if keep is not None:
            prev = prev * keep
        ext = jnp.concatenate([prev, a], axis=0)
        tail_ref[:, c0:c0 + FC_FFN] = a[tm - halo:tm]
        a1 = ext[halo - 1:halo - 1 + tm]
        a2 = ext[halo - 2:halo - 2 + tm]
        cv = (cw_ref[2:3, c0:c0 + FC_FFN] * a + cw_ref[1:2, c0:c0 + FC_FFN] * a1
              + cw_ref[0:1, c0:c0 + FC_FFN] * a2 + cb_ref[:, c0:c0 + FC_FFN])
        hmid_ref[:, c0:c0 + FC_FFN] = (cv * _sigmoid(cv) * g).astype(BF16)

    for ph in range(2):
        cur = 1 - ph
        keep = None if ph == 0 else jnp.where(lax.rem(k, pairs_per_seq) == 0, 0.0, 1.0).astype(F32)
        for c in range(n_chunk):
            mlp_chunk(c, cur, keep)
        pieces = [functools.partial(f, p * pr, ph) for p in range(tm // pr) for f in (out_piece, mix_piece)]
        hm = hmid_ref[...]
        for n in range(n_slice):
            anchor = pieces[n]()
            ffn = jnp.dot(hm, wdn_ref[:, n * ws:(n + 1) * ws], preferred_element_type=F32)
            r = alpha * x1_ref[cur, :, n * ws:(n + 1) * ws] + ffn
            zero = jnp.where(k < 0, anchor, 0.0)
            res_ref[ph, 0:8, n * ws:(n + 1) * ws] = r[0:8] + zero
            res_ref[ph, 8:tm, n * ws:(n + 1) * ws] = r[8:tm]


def _mixer_ffn(alpha, x, ys, yr, glu_w, glu_b, s5_gain, w_out, ln1_g, ln1_b,
               w_up, conv_w, conv_b, w_down, ln2_g, ln2_b):
    bsz, seq, d = x.shape
    tm = TM_FFN
    s5w, ret_w = ys.shape[-1], yr.shape[-1]
    dff = w_down.shape[0]
    pps = seq // (2 * tm)
    n_pairs = bsz * pps
    row = lambda a: a.reshape(1, -1).astype(F32)

    def tok(w, lag):
        def index(k):
            t = jnp.clip(k - lag, 0, n_pairs - 1)
            return (t // pps, t % pps, 0)
        return pl.BlockSpec((1, 2 * tm, w), index)

    operands = [x, ys, yr, glu_w.astype(BF16), row(glu_b), row(s5_gain), w_out.astype(BF16),
                row(ln1_g), row(ln1_b), w_up.astype(BF16), conv_w.astype(F32), row(conv_b),
                w_down.astype(BF16), row(ln2_g), row(ln2_b)]
    in_specs = [tok(d, 0), tok(s5w, 0), tok(ret_w, 0)] + [_const_spec(a.shape) for a in operands[3:]]
    return pl.pallas_call(
        functools.partial(_ffn_kernel, alpha, pps),
        grid=(n_pairs + 1,),
        in_specs=in_specs,
        out_specs=tok(d, 1),
        out_shape=jax.ShapeDtypeStruct((bsz, seq, d), x.dtype),
        scratch_shapes=[pltpu.VMEM((2, tm, d), F32), pltpu.VMEM((2, tm, d), BF16), pltpu.VMEM((2, tm, d), F32),
                        pltpu.VMEM((tm, dff), BF16), pltpu.VMEM((8, dff), F32)],
        compiler_params=pltpu.CompilerParams(
            dimension_semantics=("arbitrary",), vmem_limit_bytes=VMEM_LIMIT_BYTES),
        name="mixer_ffn",
    )(*operands)


def kernel(x, w_in, s5_lambda_re, s5_lambda_im, s5_b_re, s5_b_im, s5_c_re, s5_c_im, s5_d, s5_log_step,
           s5_glu_w, s5_glu_b, s5_out_gain, ret_gn_gain, w_out, ln1_g, ln1_b, ffn_w_up, ffn_conv_w,
           ffn_conv_b, ffn_w_down, ln2_g, ln2_b):
    depth = w_in.shape[0]
    alpha = (2.0 * depth) ** 0.25
    s5w = s5_d.shape[-1]
    for l in range(depth):
        u, yr = _mixer_front(x, w_in[l], ret_gn_gain[l], s5w)
        mats = _s5_matrices(s5_lambda_re[l], s5_lambda_im[l], s5_b_re[l], s5_b_im[l], s5_c_re[l],
                            s5_c_im[l], s5_d[l], s5_log_step[l], S5_Q)
        ys = _s5_scan(u, mats)
        x = _mixer_ffn(alpha, x, ys, yr, s5_glu_w[l], s5_glu_b[l], s5_out_gain[l], w_out[l], ln1_g[l],
                       ln1_b[l], ffn_w_up[l], ffn_conv_w[l], ffn_conv_b[l], ffn_w_down[l], ln2_g[l], ln2_b[l])
    return x
```

```python
import functools
import math

import numpy as np
import jax
import jax.numpy as jnp
from jax import lax
from jax.experimental import pallas as pl
from jax.experimental.pallas import tpu as pltpu

CHUNK = 64
S5_GROUP = 16
S5_STATE = 64
RET_HEADS = 4
RET_DK = 64
RET_DV = 128
CONV_W = 3
ROPE_BASE = 10000.0
LN_EPS = 1e-5

LANES = 128
SUBLANES_BF16 = 16
VMEM_LIMIT_BYTES = 56 * 1024 * 1024

TL_FRONT = 512
TB_RET = 256
S5_Q = 8
S5_NB = 32
TM_FFN = 512
FC_FFN = 256
FFN_SUB = 256
FFN_SIDE_AT = 1

F32 = jnp.float32
BF16 = jnp.bfloat16
HI = lax.Precision.HIGHEST


def _const_spec(shape):
    nd = len(shape)
    return pl.BlockSpec(shape, lambda *_: (0,) * nd, pipeline_mode=pl.Buffered(1))


def _layer_norm(v, g, b):
    mu = jnp.mean(v, axis=-1, keepdims=True)
    vc = v - mu
    var = jnp.mean(vc * vc, axis=-1, keepdims=True)
    return vc * lax.rsqrt(var + LN_EPS) * g + b


def _sigmoid(v):
    return 1.0 / (1.0 + jnp.exp(-v))


def _front_kernel(x_ref, w_ref, cos_ref, sin_ref, hm_ref, dmask_ref, xi_ref, zk_ref,
                  gdec_ref, gn_ref, u_ref, yr_ref, state_ref):
    @pl.when(pl.program_id(1) == 0)
    def _():
        state_ref[...] = jnp.zeros_like(state_ref)

    s5w = u_ref.shape[-1]
    xb = x_ref[0].astype(BF16)
    o_qk = s5w
    qk = jnp.dot(xb, w_ref[:, o_qk:o_qk + 4 * LANES], preferred_element_type=F32)
    cos = cos_ref[...]
    sin = sin_ref[...]
    q1, q2 = qk[:, 0:LANES], qk[:, LANES:2 * LANES]
    k1, k2 = qk[:, 2 * LANES:3 * LANES], qk[:, 3 * LANES:4 * LANES]
    qf = jnp.concatenate([q1 * cos - q2 * sin, q1 * sin + q2 * cos], axis=1)
    kf = jnp.concatenate([k1 * cos - k2 * sin, k1 * sin + k2 * cos], axis=1)

    o_v = o_qk + 4 * LANES
    ret_w = RET_HEADS * RET_DV
    v = jnp.dot(xb, w_ref[:, o_v:o_v + ret_w], preferred_element_type=F32).astype(BF16)
    gate = jnp.dot(xb, w_ref[:, o_v + ret_w:o_v + 2 * ret_w], preferred_element_type=F32)

    tl = xb.shape[0]
    for blk in range(tl // TB_RET):
        r0 = blk * TB_RET
        qb = qf[r0:r0 + TB_RET].astype(BF16)
        kb = kf[r0:r0 + TB_RET]
        for hd in range(RET_HEADS):
            c0 = hd * RET_DV
            kh = (kb * hm_ref[hd]).astype(BF16)
            sc = lax.dot_general(qb, kh, (((1,), (1,)), ((), ())), preferred_element_type=F32)
            p = (sc * dmask_ref[hd]).astype(BF16)
            vh = v[r0:r0 + TB_RET, c0:c0 + RET_DV]
            st = state_ref[hd]
            o = jnp.dot(p, vh, preferred_element_type=F32)
            o = o + xi_ref[hd] * jnp.dot(qb, st.astype(BF16), preferred_element_type=F32)
            kz = (kb * zk_ref[hd]).astype(BF16)
            upd = lax.dot_general(kz, vh, (((0,), (0,)), ((), ())), preferred_element_type=F32)
            state_ref[hd] = st * gdec_ref[hd] + upd
            mu = jnp.mean(o, axis=-1, keepdims=True)
            oc = o - mu
            var = jnp.mean(oc * oc, axis=-1, keepdims=True)
            on = oc * lax.rsqrt(var + LN_EPS) * gn_ref[:, c0:c0 + RET_DV]
            g = gate[r0:r0 + TB_RET, c0:c0 + RET_DV]
            yr_ref[0, r0:r0 + TB_RET, c0:c0 + RET_DV] = (on * (g * _sigmoid(g))).astype(BF16)

    u_ref[0] = jnp.dot(xb, w_ref[:, 0:s5w], preferred_element_type=F32).astype(BF16)


def _retention_consts(tb):
    log_gamma = np.log1p(-(2.0 ** (-5.0 - np.arange(RET_HEADS, dtype=np.float64))))
    pos = np.arange(tb, dtype=np.float64)
    chunk = np.arange(tb) // CHUNK
    visible = (chunk[None, :] <= chunk[:, None]).astype(np.float64)
    dist = np.abs(pos[:, None] - pos[None, :])
    dmask = np.exp(dist[None] * log_gamma[:, None, None]) * visible[None]
    xi = np.exp((pos + 1.0)[None, :] * log_gamma[:, None])
    zeta = np.exp((tb - 1.0 - pos)[None, :] * log_gamma[:, None])
    gdec = np.exp(tb * log_gamma)
    lane = np.arange(2 * LANES) % LANES
    half = RET_DK // 2
    hm = np.stack([(lane // half == h) for h in range(RET_HEADS)]).astype(np.float64) * RET_DK ** -0.5
    xi_b = np.broadcast_to(xi[:, :, None], (RET_HEADS, tb, RET_DV))
    zk = zeta[:, :, None] * hm[:, None, :]
    gdec_b = np.broadcast_to(gdec[:, None, None], (RET_HEADS, 2 * LANES, RET_DV))
    return tuple(jnp.asarray(np.ascontiguousarray(a), F32) for a in (hm[:, None, :], dmask, xi_b, zk, gdec_b))


def _front_weight(w_in, s5w):
    half = RET_DK // 2
    first = np.concatenate([np.arange(h * RET_DK, h * RET_DK + half) for h in range(RET_HEADS)])
    second = first + half
    nqk = RET_HEADS * RET_DK
    perm = np.concatenate([np.arange(s5w), s5w + first, s5w + second, s5w + nqk + first,
                           s5w + nqk + second, np.arange(s5w + 2 * nqk, w_in.shape[1])])
    return w_in[:, perm].astype(BF16)


def _mixer_front(x, w_in, gn_gain, s5w):
    bsz, seq, d = x.shape
    tl = TL_FRONT
    ret_w = RET_HEADS * RET_DV
    half = RET_DK // 2
    freqs = np.float32(ROPE_BASE) ** (-np.arange(half, dtype=np.float32) / np.float32(half))
    ang = (np.arange(seq, dtype=np.float32)[:, None] * freqs[None, :]).astype(np.float64)
    cos = jnp.asarray(np.tile(np.cos(ang), (1, RET_HEADS)), F32)
    sin = jnp.asarray(np.tile(np.sin(ang), (1, RET_HEADS)), F32)
    hm, dmask, xi_b, zk, gdec_b = _retention_consts(TB_RET)
    w = _front_weight(w_in, s5w)
    return pl.pallas_call(
        _front_kernel,
        grid=(bsz, seq // tl),
        in_specs=[
            pl.BlockSpec((1, tl, d), lambda b, i: (b, i, 0)),
            _const_spec(w.shape),
            pl.BlockSpec((tl, LANES), lambda b, i: (i, 0)),
            pl.BlockSpec((tl, LANES), lambda b, i: (i, 0)),
            _const_spec(hm.shape), _const_spec(dmask.shape), _const_spec(xi_b.shape),
            _const_spec(zk.shape), _const_spec(gdec_b.shape), _const_spec((1, ret_w)),
        ],
        out_specs=[
            pl.BlockSpec((1, tl, s5w), lambda b, i: (b, i, 0)),
            pl.BlockSpec((1, tl, ret_w), lambda b, i: (b, i, 0)),
        ],
        out_shape=[jax.ShapeDtypeStruct((bsz, seq, s5w), BF16),
                   jax.ShapeDtypeStruct((bsz, seq, ret_w), BF16)],
        scratch_shapes=[pltpu.VMEM((RET_HEADS, 2 * LANES, RET_DV), F32)],
        compiler_params=pltpu.CompilerParams(
            dimension_semantics=("arbitrary", "arbitrary"), vmem_limit_bytes=VMEM_LIMIT_BYTES),
        name="mixer_front",
    )(x, w, cos, sin, hm, dmask, xi_b, zk, gdec_b, gn_gain.reshape(1, ret_w).astype(F32))


def _same_group(shape, row_shift, col_shift):
    gpl_mask = LANES // S5_GROUP - 1
    rows = lax.broadcasted_iota(jnp.int32, shape, 0) >> row_shift
    cols = (lax.broadcasted_iota(jnp.int32, shape, 1) >> col_shift) & gpl_mask
    return (rows == cols).astype(F32)


def _s5_build(kt_ref, at_ref, ct_ref, t_ref, e_ref, c_ref):
    q = kt_ref.shape[1]
    gpl = LANES // S5_GROUP
    ch_shift = S5_GROUP.bit_length() - 1
    st_shift = S5_STATE.bit_length() - 1
    t_ref[...] = jnp.zeros_like(t_ref)
    mask_t = _same_group((LANES, LANES), ch_shift, ch_shift)
    for lag in range(q):
        bd = (jnp.concatenate([kt_ref[0, lag]] * gpl, axis=0) * mask_t).astype(BF16)
        for i in range(q - lag):
            j = i + lag
            t_ref[i * LANES:(i + 1) * LANES, j * LANES:(j + 1) * LANES] = bd
    mask_e = _same_group((LANES, e_ref.shape[1]), ch_shift, st_shift)
    for i in range(q):
        e_ref[i * LANES:(i + 1) * LANES, :] = (jnp.concatenate([at_ref[0, i]] * gpl, axis=0) * mask_e).astype(BF16)
    ns = c_ref.shape[0] // 2
    mask_c = _same_group((ns, c_ref.shape[1]), st_shift, ch_shift)
    for part in range(2):
        c_ref[part * ns:(part + 1) * ns, :] = (jnp.concatenate([ct_ref[0, part]] * gpl, axis=0) * mask_c).astype(BF16)


def _s5_kernel(u_ref, kt_ref, at_ref, ct_ref, aq_ref, y_ref, t_ref, e_ref, c_ref, sprev_ref, carry_ref):
    @pl.when(pl.program_id(1) == 0)
    def _():
        carry_ref[...] = jnp.zeros_like(carry_ref)
        _s5_build(kt_ref, at_ref, ct_ref, t_ref, e_ref, c_ref)

    nb, q, bsz, _ = u_ref.shape
    xcat = jnp.concatenate([u_ref[:, j].reshape(nb * bsz, LANES) for j in range(q)], axis=1)
    e = jnp.dot(xcat, e_ref[...], preferred_element_type=F32)
    ns = e.shape[1] // 2
    aqr = aq_ref[0, :, 0:ns]
    aqi = aq_ref[0, :, ns:2 * ns]
    s_re = carry_ref[:, 0:ns]
    s_im = carry_ref[:, ns:2 * ns]
    for n in range(nb):
        r0 = n * bsz
        sprev_ref[r0:r0 + bsz, :] = jnp.concatenate([s_re, s_im], axis=1).astype(BF16)
        e_re = e[r0:r0 + bsz, 0:ns]
        e_im = e[r0:r0 + bsz, ns:2 * ns]
        s_re, s_im = aqr * s_re - aqi * s_im + e_re, aqr * s_im + aqi * s_re + e_im
    carry_ref[...] = jnp.concatenate([s_re, s_im], axis=1)
    y = jnp.dot(xcat, t_ref[...], preferred_element_type=F32)
    y = y + jnp.dot(sprev_ref[...], c_ref[...], preferred_element_type=F32)
    for j in range(q):
        y_ref[:, j] = y[:, j * LANES:(j + 1) * LANES].reshape(nb, bsz, LANES).astype(BF16)


def _s5_matrices(lam_re, lam_im, b_re, b_im, c_re, c_im, d, log_step, q):
    g, p = lam_re.shape
    cch = b_re.shape[-1]
    gpl = LANES // cch
    nblk = g // gpl
    dt = jnp.exp(log_step)[:, None]
    zr, zi = lam_re * dt, lam_im * dt
    ks = jnp.arange(q + 1, dtype=F32)[:, None, None]
    mag = jnp.exp(ks * zr[None])
    pr, pi = mag * jnp.cos(ks * zi[None]), mag * jnp.sin(ks * zi[None])
    nr, ni = pr[1] - 1.0, pi[1]
    den = lam_re * lam_re + lam_im * lam_im
    fr = (nr * lam_re + ni * lam_im) / den
    fi = (ni * lam_re - nr * lam_im) / den
    bbr = fr[..., None] * b_re - fi[..., None] * b_im
    bbi = fr[..., None] * b_im + fi[..., None] * b_re
    wr = pr[:q, :, :, None] * bbr[None] - pi[:q, :, :, None] * bbi[None]
    wi = pr[:q, :, :, None] * bbi[None] + pi[:q, :, :, None] * bbr[None]
    kern = (jnp.einsum('gop,kgpc->kgoc', c_re, wr, precision=HI)
            - jnp.einsum('gop,kgpc->kgoc', c_im, wi, precision=HI))
    kern = kern.at[0].add(jnp.eye(cch, dtype=F32)[None] * d.reshape(g, 1, cch))
    kt = kern.reshape(q, nblk, gpl, cch, cch).transpose(1, 0, 4, 2, 3).reshape(nblk, q, cch, LANES)
    er = wr[::-1].reshape(q, nblk, gpl, p, cch).transpose(1, 0, 4, 2, 3)
    ei = wi[::-1].reshape(q, nblk, gpl, p, cch).transpose(1, 0, 4, 2, 3)
    at = jnp.concatenate([er.reshape(nblk, q, cch, gpl * p), ei.reshape(nblk, q, cch, gpl * p)], axis=3)
    cr = c_re[None] * pr[1:, :, None, :] - c_im[None] * pi[1:, :, None, :]
    ci = c_re[None] * pi[1:, :, None, :] + c_im[None] * pr[1:, :, None, :]
    cr = cr.reshape(q, nblk, gpl, cch, p).transpose(1, 4, 0, 2, 3).reshape(nblk, p, q * LANES)
    ci = ci.reshape(q, nblk, gpl, cch, p).transpose(1, 4, 0, 2, 3).reshape(nblk, p, q * LANES)
    ct = jnp.stack([cr, -ci], axis=1)
    aq = jnp.concatenate([pr[q].reshape(nblk, gpl * p), pi[q].reshape(nblk, gpl * p)], axis=1)
    return kt, at, ct, aq


def _s5_scan(u, mats):
    kt, at, ct, aq = mats
    bsz, seq, w = u.shape
    q, nb = S5_Q, S5_NB
    nblk = w // LANES
    ns2 = aq.shape[-1]
    u4 = jnp.transpose(u, (1, 0, 2)).reshape(seq // q, q, bsz, w)
    aq_b = jnp.broadcast_to(aq[:, None, :], (nblk, bsz, ns2))
    blk = pl.BlockSpec((nb, q, bsz, LANES), lambda m, i: (i, 0, 0, m))
    tab = lambda a: pl.BlockSpec((1,) + a.shape[1:], lambda m, i: (m,) + (0,) * (a.ndim - 1))
    y4 = pl.pallas_call(
        _s5_kernel,
        grid=(nblk, seq // (q * nb)),
        in_specs=[blk, tab(kt), tab(at), tab(ct), tab(aq_b)],
        out_specs=blk,
        out_shape=jax.ShapeDtypeStruct(u4.shape, BF16),
        scratch_shapes=[pltpu.VMEM((q * LANES, q * LANES), BF16),
                        pltpu.VMEM((q * LANES, ns2), BF16),
                        pltpu.VMEM((ns2, q * LANES), BF16),
                        pltpu.VMEM((nb * bsz, ns2), BF16),
                        pltpu.VMEM((bsz, ns2), F32)],
        compiler_params=pltpu.CompilerParams(
            dimension_semantics=("arbitrary", "arbitrary"), vmem_limit_bytes=VMEM_LIMIT_BYTES),
        name="s5_scan",
    )(u4, kt, at, ct, aq_b)
    return jnp.transpose(y4.reshape(seq, bsz, w), (1, 0, 2))


def _ffn_kernel(alpha, x_ref, ys_ref, yr_ref, gluw_ref, glub_ref, s5g_ref, wout_ref, ln1g_ref, ln1b_ref,
                wup_ref, cw_ref, cb_ref, wdn_ref, ln2g_ref, ln2b_ref, o_ref, hmid_ref, tail_ref):
    @pl.when(pl.program_id(1) == 0)
    def _():
        tail_ref[...] = jnp.zeros_like(tail_ref)

    tm = x_ref.shape[1]
    ts = FFN_SUB
    n_sub = tm // ts
    s5w = ys_ref.shape[-1]
    dff = hmid_ref.shape[1]
    halo = tail_ref.shape[0]

    def mix(r0):
        y = ys_ref[0, r0:r0 + ts, :].astype(F32)
        cdf = 0.5 * (1.0 + jnp.tanh(math.sqrt(2.0 / math.pi) * (y + 0.044715 * (y * y * y))))
        y = y * cdf
        z = jnp.dot(y.astype(BF16), gluw_ref[...], preferred_element_type=F32) + glub_ref[...]
        y = y * _sigmoid(z)
        y = y * lax.rsqrt(jnp.mean(y * y, axis=-1, keepdims=True) + LN_EPS) * s5g_ref[...]
        mix = jnp.dot(y.astype(BF16), wout_ref[0:s5w, :], preferred_element_type=F32)
        mix = mix + jnp.dot(yr_ref[0, r0:r0 + ts, :], wout_ref[s5w:, :], preferred_element_type=F32)
        return _layer_norm(alpha * x_ref[0, r0:r0 + ts, :] + mix, ln1g_ref[...], ln1b_ref[...])

    def chunk(r0, xb, c):
        c0 = c * FC_FFN
        a = jnp.dot(xb, wup_ref[:, c0:c0 + FC_FFN], preferred_element_type=F32)
        g = jnp.dot(xb, wup_ref[:, dff + c0:dff + c0 + FC_FFN], preferred_element_type=F32)
        ext = jnp.concatenate([tail_ref[:, c0:c0 + FC_FFN], a], axis=0)
        tail_ref[:, c0:c0 + FC_FFN] = a[ts - halo:ts]
        a1 = ext[halo - 1:halo - 1 + ts]
        a2 = ext[halo - 2:halo - 2 + ts]
        cv = (cw_ref[2:3, c0:c0 + FC_FFN] * a + cw_ref[1:2, c0:c0 + FC_FFN] * a1
              + cw_ref[0:1, c0:c0 + FC_FFN] * a2 + cb_ref[:, c0:c0 + FC_FFN])
        hmid_ref[r0:r0 + ts, c0:c0 + FC_FFN] = (cv * _sigmoid(cv) * g).astype(BF16)

    def out(r0, res):
        o_ref[0, r0:r0 + ts, :] = _layer_norm(res, ln2g_ref[...], ln2b_ref[...])

    x1 = mix(0)
    res_prev = None
    for sub in range(n_sub):
        r0 = sub * ts
        xb = x1.astype(BF16)
        x1_next = None
        for c in range(dff // FC_FFN):
            chunk(r0, xb, c)
            if c == FFN_SIDE_AT:
                if sub + 1 < n_sub:
                    x1_next = mix(r0 + ts)
                if res_prev is not None:
                    out(r0 - ts, res_prev)
        res_prev = alpha * x1 + jnp.dot(hmid_ref[r0:r0 + ts, :], wdn_ref[...], preferred_element_type=F32)
        x1 = x1_next
    out((n_sub - 1) * ts, res_prev)


def _mixer_ffn(alpha, x, ys, yr, glu_w, glu_b, s5_gain, w_out, ln1_g, ln1_b,
               w_up, conv_w, conv_b, w_down, ln2_g, ln2_b):
    bsz, seq, d = x.shape
    tm = TM_FFN
    s5w, ret_w = ys.shape[-1], yr.shape[-1]
    dff = w_down.shape[0]
    row = lambda a: a.reshape(1, -1).astype(F32)
    tok = lambda w: pl.BlockSpec((1, tm, w), lambda b, i: (b, i, 0))
    operands = [x, ys, yr, glu_w.astype(BF16), row(glu_b), row(s5_gain), w_out.astype(BF16),
                row(ln1_g), row(ln1_b), w_up.astype(BF16), conv_w.astype(F32), row(conv_b),
                w_down.astype(BF16), row(ln2_g), row(ln2_b)]
    in_specs = [tok(d), tok(s5w), tok(ret_w)] + [_const_spec(a.shape) for a in operands[3:]]
    return pl.pallas_call(
        functools.partial(_ffn_kernel, alpha),
        grid=(bsz, seq // tm),
        in_specs=in_specs,
        out_specs=tok(d),
        out_shape=jax.ShapeDtypeStruct((bsz, seq, d), x.dtype),
        scratch_shapes=[pltpu.VMEM((tm, dff), BF16), pltpu.VMEM((8, dff), F32)],
        compiler_params=pltpu.CompilerParams(
            dimension_semantics=("arbitrary", "arbitrary"), vmem_limit_bytes=VMEM_LIMIT_BYTES),
        name="mixer_ffn",
    )(*operands)


def kernel(x, w_in, s5_lambda_re, s5_lambda_im, s5_b_re, s5_b_im, s5_c_re, s5_c_im, s5_d, s5_log_step,
           s5_glu_w, s5_glu_b, s5_out_gain, ret_gn_gain, w_out, ln1_g, ln1_b, ffn_w_up, ffn_conv_w,
           ffn_conv_b, ffn_w_down, ln2_g, ln2_b):
    depth = w_in.shape[0]
    alpha = (2.0 * depth) ** 0.25
    s5w = s5_d.shape[-1]
    for l in range(depth):
        u, yr = _mixer_front(x, w_in[l], ret_gn_gain[l], s5w)
        mats = _s5_matrices(s5_lambda_re[l], s5_lambda_im[l], s5_b_re[l], s5_b_im[l], s5_c_re[l],
                            s5_c_im[l], s5_d[l], s5_log_step[l], S5_Q)
        ys = _s5_scan(u, mats)
        x = _mixer_ffn(alpha, x, ys, yr, s5_glu_w[l], s5_glu_b[l], s5_out_gain[l], w_out[l], ln1_g[l],
                       ln1_b[l], ffn_w_up[l], ffn_conv_w[l], ffn_conv_b[l], ffn_w_down[l], ln2_g[l], ln2_b[l])
    return x
```

```python
import functools
import math

import numpy as np
import jax
import jax.numpy as jnp
from jax import lax
from jax.experimental import pallas as pl
from jax.experimental.pallas import tpu as pltpu

CHUNK = 64
S5_GROUP = 16
S5_STATE = 64
RET_HEADS = 4
RET_DK = 64
RET_DV = 128
CONV_W = 3
ROPE_BASE = 10000.0
LN_EPS = 1e-5

LANES = 128
SUBLANES_BF16 = 16
VMEM_LIMIT_BYTES = 56 * 1024 * 1024

TL_FRONT = 1024
TB_RET = 256
S5_Q = 8
S5_NB = 32
TM_FFN = 1024
FC_FFN = 256
FFN_SUB = 256
FFN_MIX_AT = 1
FFN_OUT_AT = 6

F32 = jnp.float32
BF16 = jnp.bfloat16
HI = lax.Precision.HIGHEST


def _const_spec(shape):
    nd = len(shape)
    return pl.BlockSpec(shape, lambda *_: (0,) * nd, pipeline_mode=pl.Buffered(1))


def _layer_norm(v, g, b):
    mu = jnp.mean(v, axis=-1, keepdims=True)
    vc = v - mu
    var = jnp.mean(vc * vc, axis=-1, keepdims=True)
    return vc * lax.rsqrt(var + LN_EPS) * g + b


def _sigmoid(v):
    return 1.0 / (1.0 + jnp.exp(-v))


def _front_kernel(x_ref, w_ref, cos_ref, sin_ref, hm_ref, dmask_ref, xi_ref, zk_ref,
                  gdec_ref, smask_ref, gn_ref, u_ref, yr_ref, state_ref):
    @pl.when(pl.program_id(1) == 0)
    def _():
        state_ref[...] = jnp.zeros_like(state_ref)

    s5w = u_ref.shape[-1]
    xb = x_ref[0].astype(BF16)
    o_qk = s5w
    qk = jnp.dot(xb, w_ref[:, o_qk:o_qk + 4 * LANES], preferred_element_type=F32)
    cos = cos_ref[...]
    sin = sin_ref[...]
    q1, q2 = qk[:, 0:LANES], qk[:, LANES:2 * LANES]
    k1, k2 = qk[:, 2 * LANES:3 * LANES], qk[:, 3 * LANES:4 * LANES]
    qf = jnp.concatenate([q1 * cos - q2 * sin, q1 * sin + q2 * cos], axis=1)
    kf = jnp.concatenate([k1 * cos - k2 * sin, k1 * sin + k2 * cos], axis=1)

    o_v = o_qk + 4 * LANES
    ret_w = RET_HEADS * RET_DV
    v = jnp.dot(xb, w_ref[:, o_v:o_v + ret_w], preferred_element_type=F32).astype(BF16)
    gate = jnp.dot(xb, w_ref[:, o_v + ret_w:o_v + 2 * ret_w], preferred_element_type=F32)

    tl = xb.shape[0]
    pw = 2 * RET_DV
    for blk in range(tl // TB_RET):
        r0 = blk * TB_RET
        qb = qf[r0:r0 + TB_RET].astype(BF16)
        kb = kf[r0:r0 + TB_RET]
        for pair in range(RET_HEADS // 2):
            p0 = pair * pw
            vp = v[r0:r0 + TB_RET, p0:p0 + pw]
            st = state_ref[pair]
            cross = xi_ref[pair] * jnp.dot(qb, st.astype(BF16), preferred_element_type=F32)
            kz = (kb * zk_ref[pair]).astype(BF16)
            upd = lax.dot_general(kz, vp, (((0,), (0,)), ((), ())), preferred_element_type=F32)
            state_ref[pair] = st * gdec_ref[pair] + upd * smask_ref[pair]
            for sub in range(2):
                hd = 2 * pair + sub
                c0 = hd * RET_DV
                kh = (kb * hm_ref[hd]).astype(BF16)
                sc = lax.dot_general(qb, kh, (((1,), (1,)), ((), ())), preferred_element_type=F32)
                p = (sc * dmask_ref[hd]).astype(BF16)
                o = jnp.dot(p, v[r0:r0 + TB_RET, c0:c0 + RET_DV], preferred_element_type=F32)
                o = o + cross[:, sub * RET_DV:(sub + 1) * RET_DV]
                mu = jnp.mean(o, axis=-1, keepdims=True)
                oc = o - mu
                var = jnp.mean(oc * oc, axis=-1, keepdims=True)
                on = oc * lax.rsqrt(var + LN_EPS) * gn_ref[:, c0:c0 + RET_DV]
                g = gate[r0:r0 + TB_RET, c0:c0 + RET_DV]
                yr_ref[0, r0:r0 + TB_RET, c0:c0 + RET_DV] = (on * (g * _sigmoid(g))).astype(BF16)

    u_ref[0] = jnp.dot(xb, w_ref[:, 0:s5w], preferred_element_type=F32).astype(BF16)


def _retention_consts(tb):
    log_gamma = np.log1p(-(2.0 ** (-5.0 - np.arange(RET_HEADS, dtype=np.float64))))
    pos = np.arange(tb, dtype=np.float64)
    chunk = np.arange(tb) // CHUNK
    visible = (chunk[None, :] <= chunk[:, None]).astype(np.float64)
    dist = np.abs(pos[:, None] - pos[None, :])
    dmask = np.exp(dist[None] * log_gamma[:, None, None]) * visible[None]
    xi = np.exp((pos + 1.0)[None, :] * log_gamma[:, None])
    zeta = np.exp((tb - 1.0 - pos)[None, :] * log_gamma[:, None])
    gdec = np.exp(tb * log_gamma)
    lane = np.arange(2 * LANES) % LANES
    half = RET_DK // 2
    hm = np.stack([(lane // half == h) for h in range(RET_HEADS)]).astype(np.float64) * RET_DK ** -0.5
    npair = RET_HEADS // 2
    own = (hm > 0).astype(np.float64)
    xi_p = np.repeat(xi[:, :, None], RET_DV, axis=2).reshape(npair, 2, tb, RET_DV)
    xi_p = np.concatenate([xi_p[:, 0], xi_p[:, 1]], axis=2)
    zk = (zeta[:, :, None] * hm[:, None, :]).reshape(npair, 2, tb, 2 * LANES).sum(axis=1)
    col_head = np.repeat(np.arange(RET_HEADS).reshape(npair, 2), RET_DV, axis=1)
    smask = np.stack([own[col_head[p]].T for p in range(npair)])
    gdec_p = smask * gdec[col_head][:, None, :]
    return tuple(jnp.asarray(np.ascontiguousarray(a), F32)
                 for a in (hm[:, None, :], dmask, xi_p, zk, gdec_p, smask))


def _front_weight(w_in, s5w):
    half = RET_DK // 2
    first = np.concatenate([np.arange(h * RET_DK, h * RET_DK + half) for h in range(RET_HEADS)])
    second = first + half
    nqk = RET_HEADS * RET_DK
    perm = np.concatenate([np.arange(s5w), s5w + first, s5w + second, s5w + nqk + first,
                           s5w + nqk + second, np.arange(s5w + 2 * nqk, w_in.shape[1])])
    return w_in[:, perm].astype(BF16)


def _mixer_front(x, w_in, gn_gain, s5w):
    bsz, seq, d = x.shape
    tl = TL_FRONT
    ret_w = RET_HEADS * RET_DV
    half = RET_DK // 2
    freqs = np.float32(ROPE_BASE) ** (-np.arange(half, dtype=np.float32) / np.float32(half))
    ang = (np.arange(seq, dtype=np.float32)[:, None] * freqs[None, :]).astype(np.float64)
    cos = jnp.asarray(np.tile(np.cos(ang), (1, RET_HEADS)), F32)
    sin = jnp.asarray(np.tile(np.sin(ang), (1, RET_HEADS)), F32)
    hm, dmask, xi_p, zk, gdec_p, smask = _retention_consts(TB_RET)
    w = _front_weight(w_in, s5w)
    return pl.pallas_call(
        _front_kernel,
        grid=(bsz, seq // tl),
        in_specs=[
            pl.BlockSpec((1, tl, d), lambda b, i: (b, i, 0)),
            _const_spec(w.shape),
            pl.BlockSpec((tl, LANES), lambda b, i: (i, 0)),
            pl.BlockSpec((tl, LANES), lambda b, i: (i, 0)),
            _const_spec(hm.shape), _const_spec(dmask.shape), _const_spec(xi_p.shape),
            _const_spec(zk.shape), _const_spec(gdec_p.shape), _const_spec(smask.shape), _const_spec((1, ret_w)),
        ],
        out_specs=[
            pl.BlockSpec((1, tl, s5w), lambda b, i: (b, i, 0)),
            pl.BlockSpec((1, tl, ret_w), lambda b, i: (b, i, 0)),
        ],
        out_shape=[jax.ShapeDtypeStruct((bsz, seq, s5w), BF16),
                   jax.ShapeDtypeStruct((bsz, seq, ret_w), BF16)],
        scratch_shapes=[pltpu.VMEM((RET_HEADS // 2, 2 * LANES, 2 * RET_DV), F32)],
        compiler_params=pltpu.CompilerParams(
            dimension_semantics=("arbitrary", "arbitrary"), vmem_limit_bytes=VMEM_LIMIT_BYTES),
        name="mixer_front",
    )(x, w, cos, sin, hm, dmask, xi_p, zk, gdec_p, smask, gn_gain.reshape(1, ret_w).astype(F32))


def _same_group(shape, row_shift, col_shift):
    gpl_mask = LANES // S5_GROUP - 1
    rows = lax.broadcasted_iota(jnp.int32, shape, 0) >> row_shift
    cols = (lax.broadcasted_iota(jnp.int32, shape, 1) >> col_shift) & gpl_mask
    return (rows == cols).astype(F32)


def _s5_build(kt_ref, at_ref, ct_ref, t_ref, e_ref, c_ref):
    q = kt_ref.shape[1]
    gpl = LANES // S5_GROUP
    ch_shift = S5_GROUP.bit_length() - 1
    st_shift = S5_STATE.bit_length() - 1
    t_ref[...] = jnp.zeros_like(t_ref)
    mask_t = _same_group((LANES, LANES), ch_shift, ch_shift)
    for lag in range(q):
        bd = (jnp.concatenate([kt_ref[0, lag]] * gpl, axis=0) * mask_t).astype(BF16)
        for i in range(q - lag):
            j = i + lag
            t_ref[i * LANES:(i + 1) * LANES, j * LANES:(j + 1) * LANES] = bd
    mask_e = _same_group((LANES, e_ref.shape[1]), ch_shift, st_shift)
    for i in range(q):
        e_ref[i * LANES:(i + 1) * LANES, :] = (jnp.concatenate([at_ref[0, i]] * gpl, axis=0) * mask_e).astype(BF16)
    ns = c_ref.shape[0] // 2
    mask_c = _same_group((ns, c_ref.shape[1]), st_shift, ch_shift)
    for part in range(2):
        c_ref[part * ns:(part + 1) * ns, :] = (jnp.concatenate([ct_ref[0, part]] * gpl, axis=0) * mask_c).astype(BF16)


def _s5_kernel(u_ref, kt_ref, at_ref, ct_ref, aq_ref, y_ref, t_ref, e_ref, c_ref, sprev_ref, carry_ref):
    @pl.when(pl.program_id(1) == 0)
    def _():
        carry_ref[...] = jnp.zeros_like(carry_ref)
        _s5_build(kt_ref, at_ref, ct_ref, t_ref, e_ref, c_ref)

    nb, q, bsz, _ = u_ref.shape
    xcat = jnp.concatenate([u_ref[:, j].reshape(nb * bsz, LANES) for j in range(q)], axis=1)
    e = jnp.dot(xcat, e_ref[...], preferred_element_type=F32)
    ns = e.shape[1] // 2
    aqr = aq_ref[0, :, 0:ns]
    aqi = aq_ref[0, :, ns:2 * ns]
    s_re = carry_ref[:, 0:ns]
    s_im = carry_ref[:, ns:2 * ns]
    for n in range(nb):
        r0 = n * bsz
        sprev_ref[r0:r0 + bsz, :] = jnp.concatenate([s_re, s_im], axis=1).astype(BF16)
        e_re = e[r0:r0 + bsz, 0:ns]
        e_im = e[r0:r0 + bsz, ns:2 * ns]
        s_re, s_im = aqr * s_re - aqi * s_im + e_re, aqr * s_im + aqi * s_re + e_im
    carry_ref[...] = jnp.concatenate([s_re, s_im], axis=1)
    tw = 2 * LANES
    for jt in range(q * LANES // tw):
        hi = (jt + 1) * tw
        y = jnp.dot(xcat[:, 0:hi], t_ref[0:hi, jt * tw:hi], preferred_element_type=F32)
        y = y + jnp.dot(sprev_ref[...], c_ref[:, jt * tw:hi], preferred_element_type=F32)
        for j in range(jt * tw // LANES, hi // LANES):
            yj = y[:, j * LANES - jt * tw:(j + 1) * LANES - jt * tw]
            y_ref[:, j] = yj.reshape(nb, bsz, LANES).astype(BF16)


def _s5_matrices(lam_re, lam_im, b_re, b_im, c_re, c_im, d, log_step, q):
    g, p = lam_re.shape
    cch = b_re.shape[-1]
    gpl = LANES // cch
    nblk = g // gpl
    dt = jnp.exp(log_step)[:, None]
    zr, zi = lam_re * dt, lam_im * dt
    ks = jnp.arange(q + 1, dtype=F32)[:, None, None]
    mag = jnp.exp(ks * zr[None])
    pr, pi = mag * jnp.cos(ks * zi[None]), mag * jnp.sin(ks * zi[None])
    nr, ni = pr[1] - 1.0, pi[1]
    den = lam_re * lam_re + lam_im * lam_im
    fr = (nr * lam_re + ni * lam_im) / den
    fi = (ni * lam_re - nr * lam_im) / den
    bbr = fr[..., None] * b_re - fi[..., None] * b_im
    bbi = fr[..., None] * b_im + fi[..., None] * b_re
    wr = pr[:q, :, :, None] * bbr[None] - pi[:q, :, :, None] * bbi[None]
    wi = pr[:q, :, :, None] * bbi[None] + pi[:q, :, :, None] * bbr[None]
    kern = (jnp.einsum('gop,kgpc->kgoc', c_re, wr, precision=HI)
            - jnp.einsum('gop,kgpc->kgoc', c_im, wi, precision=HI))
    kern = kern.at[0].add(jnp.eye(cch, dtype=F32)[None] * d.reshape(g, 1, cch))
    kt = kern.reshape(q, nblk, gpl, cch, cch).transpose(1, 0, 4, 2, 3).reshape(nblk, q, cch, LANES)
    er = wr[::-1].reshape(q, nblk, gpl, p, cch).transpose(1, 0, 4, 2, 3)
    ei = wi[::-1].reshape(q, nblk, gpl, p, cch).transpose(1, 0, 4, 2, 3)
    at = jnp.concatenate([er.reshape(nblk, q, cch, gpl * p), ei.reshape(nblk, q, cch, gpl * p)], axis=3)
    cr = c_re[None] * pr[1:, :, None, :] - c_im[None] * pi[1:, :, None, :]
    ci = c_re[None] * pi[1:, :, None, :] + c_im[None] * pr[1:, :, None, :]
    cr = cr.reshape(q, nblk, gpl, cch, p).transpose(1, 4, 0, 2, 3).reshape(nblk, p, q * LANES)
    ci = ci.reshape(q, nblk, gpl, cch, p).transpose(1, 4, 0, 2, 3).reshape(nblk, p, q * LANES)
    ct = jnp.stack([cr, -ci], axis=1)
    aq = jnp.concatenate([pr[q].reshape(nblk, gpl * p), pi[q].reshape(nblk, gpl * p)], axis=1)
    return kt, at, ct, aq


def _s5_scan(u, mats):
    kt, at, ct, aq = mats
    bsz, seq, w = u.shape
    q, nb = S5_Q, S5_NB
    nblk = w // LANES
    ns2 = aq.shape[-1]
    u4 = jnp.transpose(u, (1, 0, 2)).reshape(seq // q, q, bsz, w)
    aq_b = jnp.broadcast_to(aq[:, None, :], (nblk, bsz, ns2))
    blk = pl.BlockSpec((nb, q, bsz, LANES), lambda m, i: (i, 0, 0, m))
    tab = lambda a: pl.BlockSpec((1,) + a.shape[1:], lambda m, i: (m,) + (0,) * (a.ndim - 1))
    y4 = pl.pallas_call(
        _s5_kernel,
        grid=(nblk, seq // (q * nb)),
        in_specs=[blk, tab(kt), tab(at), tab(ct), tab(aq_b)],
        out_specs=blk,
        out_shape=jax.ShapeDtypeStruct(u4.shape, BF16),
        scratch_shapes=[pltpu.VMEM((q * LANES, q * LANES), BF16),
                        pltpu.VMEM((q * LANES, ns2), BF16),
                        pltpu.VMEM((ns2, q * LANES), BF16),
                        pltpu.VMEM((nb * bsz, ns2), BF16),
                        pltpu.VMEM((bsz, ns2), F32)],
        compiler_params=pltpu.CompilerParams(
            dimension_semantics=("arbitrary", "arbitrary"), vmem_limit_bytes=VMEM_LIMIT_BYTES),
        name="s5_scan",
    )(u4, kt, at, ct, aq_b)
    return jnp.transpose(y4.reshape(seq, bsz, w), (1, 0, 2))


def _ffn_kernel(alpha, x_ref, ys_ref, yr_ref, gluw_ref, glub_ref, s5g_ref, wout_ref, ln1g_ref, ln1b_ref,
                wup_ref, cw_ref, cb_ref, wdn_ref, ln2g_ref, ln2b_ref, o_ref, hmid_ref, tail_ref):
    @pl.when(pl.program_id(1) == 0)
    def _():
        tail_ref[...] = jnp.zeros_like(tail_ref)

    tm = x_ref.shape[1]
    ts = FFN_SUB
    n_sub = tm // ts
    s5w = ys_ref.shape[-1]
    dff = hmid_ref.shape[1]
    halo = tail_ref.shape[0]

    def mix(r0):
        y = ys_ref[0, r0:r0 + ts, :].astype(F32)
        cdf = 0.5 * (1.0 + jnp.tanh(math.sqrt(2.0 / math.pi) * (y + 0.044715 * (y * y * y))))
        y = y * cdf
        z = jnp.dot(y.astype(BF16), gluw_ref[...], preferred_element_type=F32) + glub_ref[...]
        y = y * _sigmoid(z)
        y = y * lax.rsqrt(jnp.mean(y * y, axis=-1, keepdims=True) + LN_EPS) * s5g_ref[...]
        mix = jnp.dot(y.astype(BF16), wout_ref[0:s5w, :], preferred_element_type=F32)
        mix = mix + jnp.dot(yr_ref[0, r0:r0 + ts, :], wout_ref[s5w:, :], preferred_element_type=F32)
        return _layer_norm(alpha * x_ref[0, r0:r0 + ts, :] + mix, ln1g_ref[...], ln1b_ref[...])

    def chunk(r0, xb, c):
        c0 = c * FC_FFN
        a = jnp.dot(xb, wup_ref[:, c0:c0 + FC_FFN], preferred_element_type=F32)
        g = jnp.dot(xb, wup_ref[:, dff + c0:dff + c0 + FC_FFN], preferred_element_type=F32)
        ext = jnp.concatenate([tail_ref[:, c0:c0 + FC_FFN], a], axis=0)
        tail_ref[:, c0:c0 + FC_FFN] = a[ts - halo:ts]
        a1 = ext[halo - 1:halo - 1 + ts]
        a2 = ext[halo - 2:halo - 2 + ts]
        cv = (cw_ref[2:3, c0:c0 + FC_FFN] * a + cw_ref[1:2, c0:c0 + FC_FFN] * a1
              + cw_ref[0:1, c0:c0 + FC_FFN] * a2 + cb_ref[:, c0:c0 + FC_FFN])
        hmid_ref[r0:r0 + ts, c0:c0 + FC_FFN] = (cv * _sigmoid(cv) * g).astype(BF16)

    def out(r0, res):
        o_ref[0, r0:r0 + ts, :] = _layer_norm(res, ln2g_ref[...], ln2b_ref[...])

    x1 = mix(0)
    res_prev = None
    for sub in range(n_sub):
        r0 = sub * ts
        xb = x1.astype(BF16)
        x1_next = None
        for c in range(dff // FC_FFN):
            chunk(r0, xb, c)
            if c == FFN_MIX_AT and sub + 1 < n_sub:
                x1_next = mix(r0 + ts)
            if c == FFN_OUT_AT and res_prev is not None:
                out(r0 - ts, res_prev)
        res_prev = alpha * x1 + jnp.dot(hmid_ref[r0:r0 + ts, :], wdn_ref[...], preferred_element_type=F32)
        x1 = x1_next
    out((n_sub - 1) * ts, res_prev)


def _mixer_ffn(alpha, x, ys, yr, glu_w, glu_b, s5_gain, w_out, ln1_g, ln1_b,
               w_up, conv_w, conv_b, w_down, ln2_g, ln2_b):
    bsz, seq, d = x.shape
    tm = TM_FFN
    s5w, ret_w = ys.shape[-1], yr.shape[-1]
    dff = w_down.shape[0]
    row = lambda a: a.reshape(1, -1).astype(F32)
    tok = lambda w: pl.BlockSpec((1, tm, w), lambda b, i: (b, i, 0))
    operands = [x, ys, yr, glu_w.astype(BF16), row(glu_b), row(s5_gain), w_out.astype(BF16),
                row(ln1_g), row(ln1_b), w_up.astype(BF16), conv_w.astype(F32), row(conv_b),
                w_down.astype(BF16), row(ln2_g), row(ln2_b)]
    in_specs = [tok(d), tok(s5w), tok(ret_w)] + [_const_spec(a.shape) for a in operands[3:]]
    return pl.pallas_call(
        functools.partial(_ffn_kernel, alpha),
        grid=(bsz, seq // tm),
        in_specs=in_specs,
        out_specs=tok(d),
        out_shape=jax.ShapeDtypeStruct((bsz, seq, d), x.dtype),
        scratch_shapes=[pltpu.VMEM((tm, dff), BF16), pltpu.VMEM((8, dff), F32)],
        compiler_params=pltpu.CompilerParams(
            dimension_semantics=("arbitrary", "arbitrary"), vmem_limit_bytes=VMEM_LIMIT_BYTES),
        name="mixer_ffn",
    )(*operands)


def kernel(x, w_in, s5_lambda_re, s5_lambda_im, s5_b_re, s5_b_im, s5_c_re, s5_c_im, s5_d, s5_log_step,
           s5_glu_w, s5_glu_b, s5_out_gain, ret_gn_gain, w_out, ln1_g, ln1_b, ffn_w_up, ffn_conv_w,
           ffn_conv_b, ffn_w_down, ln2_g, ln2_b):
    depth = w_in.shape[0]
    alpha = (2.0 * depth) ** 0.25
    s5w = s5_d.shape[-1]
    for l in range(depth):
        u, yr = _mixer_front(x, w_in[l], ret_gn_gain[l], s5w)
        mats = _s5_matrices(s5_lambda_re[l], s5_lambda_im[l], s5_b_re[l], s5_b_im[l], s5_c_re[l],
                            s5_c_im[l], s5_d[l], s5_log_step[l], S5_Q)
        ys = _s5_scan(u, mats)
        x = _mixer_ffn(alpha, x, ys, yr, s5_glu_w[l], s5_glu_b[l], s5_out_gain[l], w_out[l], ln1_g[l],
                       ln1_b[l], ffn_w_up[l], ffn_conv_w[l], ffn_conv_b[l], ffn_w_down[l], ln2_g[l], ln2_b[l])
    return x
```

```python
import functools
import math

import numpy as np
import jax
import jax.numpy as jnp
from jax import lax
from jax.experimental import pallas as pl
from jax.experimental.pallas import tpu as pltpu

CHUNK = 64
S5_GROUP = 16
S5_STATE = 64
RET_HEADS = 4
RET_DK = 64
RET_DV = 128
CONV_W = 3
ROPE_BASE = 10000.0
LN_EPS = 1e-5

LANES = 128
SUBLANES_BF16 = 16
VMEM_LIMIT_BYTES = 56 * 1024 * 1024

TL_FRONT = 1024
TB_RET = 256
S5_Q = 8
S5_NB = 64
TM_FFN = 512
FC_FFN = 256
FFN_SUB = 256
FFN_MIX_AT = 1
FFN_OUT_AT = 6

F32 = jnp.float32
BF16 = jnp.bfloat16
HI = lax.Precision.HIGHEST


def _const_spec(shape):
    nd = len(shape)
    return pl.BlockSpec(shape, lambda *_: (0,) * nd, pipeline_mode=pl.Buffered(1))


def _layer_norm(v, g, b):
    mu = jnp.mean(v, axis=-1, keepdims=True)
    vc = v - mu
    var = jnp.mean(vc * vc, axis=-1, keepdims=True)
    return vc * lax.rsqrt(var + LN_EPS) * g + b


def _sigmoid(v):
    return 1.0 / (1.0 + jnp.exp(-v))


def _front_kernel(x_ref, w_ref, cos_ref, sin_ref, hm_ref, dmask_ref, xi_ref, zk_ref,
                  gdec_ref, smask_ref, gn_ref, u_ref, yr_ref, state_ref):
    @pl.when(pl.program_id(1) == 0)
    def _():
        state_ref[...] = jnp.zeros_like(state_ref)

    s5w = u_ref.shape[-1]
    xb = x_ref[0].astype(BF16)
    o_qk = s5w
    qk = jnp.dot(xb, w_ref[:, o_qk:o_qk + 4 * LANES], preferred_element_type=F32)
    cos = cos_ref[...]
    sin = sin_ref[...]
    q1, q2 = qk[:, 0:LANES], qk[:, LANES:2 * LANES]
    k1, k2 = qk[:, 2 * LANES:3 * LANES], qk[:, 3 * LANES:4 * LANES]
    qf = jnp.concatenate([q1 * cos - q2 * sin, q1 * sin + q2 * cos], axis=1)
    kf = jnp.concatenate([k1 * cos - k2 * sin, k1 * sin + k2 * cos], axis=1)

    o_v = o_qk + 4 * LANES
    ret_w = RET_HEADS * RET_DV
    v = jnp.dot(xb, w_ref[:, o_v:o_v + ret_w], preferred_element_type=F32).astype(BF16)
    gate = jnp.dot(xb, w_ref[:, o_v + ret_w:o_v + 2 * ret_w], preferred_element_type=F32)

    tl = xb.shape[0]
    pw = 2 * RET_DV
    for blk in range(tl // TB_RET):
        r0 = blk * TB_RET
        qb = qf[r0:r0 + TB_RET].astype(BF16)
        kb = kf[r0:r0 + TB_RET]
        for pair in range(RET_HEADS // 2):
            p0 = pair * pw
            vp = v[r0:r0 + TB_RET, p0:p0 + pw]
            st = state_ref[pair]
            cross = xi_ref[pair] * jnp.dot(qb, st.astype(BF16), preferred_element_type=F32)
            kz = (kb * zk_ref[pair]).astype(BF16)
            upd = lax.dot_general(kz, vp, (((0,), (0,)), ((), ())), preferred_element_type=F32)
            state_ref[pair] = st * gdec_ref[pair] + upd * smask_ref[pair]
            for sub in range(2):
                hd = 2 * pair + sub
                c0 = hd * RET_DV
                kh = (kb * hm_ref[hd]).astype(BF16)
                sc = lax.dot_general(qb, kh, (((1,), (1,)), ((), ())), preferred_element_type=F32)
                p = (sc * dmask_ref[hd]).astype(BF16)
                o = jnp.dot(p, v[r0:r0 + TB_RET, c0:c0 + RET_DV], preferred_element_type=F32)
                o = o + cross[:, sub * RET_DV:(sub + 1) * RET_DV]
                mu = jnp.mean(o, axis=-1, keepdims=True)
                oc = o - mu
                var = jnp.mean(oc * oc, axis=-1, keepdims=True)
                on = oc * lax.rsqrt(var + LN_EPS) * gn_ref[:, c0:c0 + RET_DV]
                g = gate[r0:r0 + TB_RET, c0:c0 + RET_DV]
                yr_ref[0, r0:r0 + TB_RET, c0:c0 + RET_DV] = (on * (g * _sigmoid(g))).astype(BF16)

    u_ref[0] = jnp.dot(xb, w_ref[:, 0:s5w], preferred_element_type=F32).astype(BF16)


def _retention_consts(tb):
    log_gamma = np.log1p(-(2.0 ** (-5.0 - np.arange(RET_HEADS, dtype=np.float64))))
    pos = np.arange(tb, dtype=np.float64)
    chunk = np.arange(tb) // CHUNK
    visible = (chunk[None, :] <= chunk[:, None]).astype(np.float64)
    dist = np.abs(pos[:, None] - pos[None, :])
    dmask = np.exp(dist[None] * log_gamma[:, None, None]) * visible[None]
    xi = np.exp((pos + 1.0)[None, :] * log_gamma[:, None])
    zeta = np.exp((tb - 1.0 - pos)[None, :] * log_gamma[:, None])
    gdec = np.exp(tb * log_gamma)
    lane = np.arange(2 * LANES) % LANES
    half = RET_DK // 2
    hm = np.stack([(lane // half == h) for h in range(RET_HEADS)]).astype(np.float64) * RET_DK ** -0.5
    npair = RET_HEADS // 2
    own = (hm > 0).astype(np.float64)
    xi_p = np.repeat(xi[:, :, None], RET_DV, axis=2).reshape(npair, 2, tb, RET_DV)
    xi_p = np.concatenate([xi_p[:, 0], xi_p[:, 1]], axis=2)
    zk = (zeta[:, :, None] * hm[:, None, :]).reshape(npair, 2, tb, 2 * LANES).sum(axis=1)
    col_head = np.repeat(np.arange(RET_HEADS).reshape(npair, 2), RET_DV, axis=1)
    smask = np.stack([own[col_head[p]].T for p in range(npair)])
    gdec_p = smask * gdec[col_head][:, None, :]
    return tuple(jnp.asarray(np.ascontiguousarray(a), F32)
                 for a in (hm[:, None, :], dmask, xi_p, zk, gdec_p, smask))


def _front_weight(w_in, s5w):
    half = RET_DK // 2
    nqk = RET_HEADS * RET_DK
    w = w_in.astype(BF16)

    def regroup(cols):
        return cols.reshape(-1, RET_HEADS, 2, half).transpose(0, 2, 1, 3).reshape(-1, nqk)

    return jnp.concatenate([w[:, :s5w], regroup(w[:, s5w:s5w + nqk]),
                            regroup(w[:, s5w + nqk:s5w + 2 * nqk]), w[:, s5w + 2 * nqk:]], axis=1)


def _mixer_front(x, w_in, gn_gain, s5w):
    bsz, seq, d = x.shape
    tl = TL_FRONT
    ret_w = RET_HEADS * RET_DV
    half = RET_DK // 2
    freqs = np.float32(ROPE_BASE) ** (-np.arange(half, dtype=np.float32) / np.float32(half))
    ang = (np.arange(seq, dtype=np.float32)[:, None] * freqs[None, :]).astype(np.float64)
    cos = jnp.asarray(np.tile(np.cos(ang), (1, RET_HEADS)), F32)
    sin = jnp.asarray(np.tile(np.sin(ang), (1, RET_HEADS)), F32)
    hm, dmask, xi_p, zk, gdec_p, smask = _retention_consts(TB_RET)
    w = _front_weight(w_in, s5w)
    return pl.pallas_call(
        _front_kernel,
        grid=(bsz, seq // tl),
        in_specs=[
            pl.BlockSpec((1, tl, d), lambda b, i: (b, i, 0)),
            _const_spec(w.shape),
            pl.BlockSpec((tl, LANES), lambda b, i: (i, 0)),
            pl.BlockSpec((tl, LANES), lambda b, i: (i, 0)),
            _const_spec(hm.shape), _const_spec(dmask.shape), _const_spec(xi_p.shape),
            _const_spec(zk.shape), _const_spec(gdec_p.shape), _const_spec(smask.shape), _const_spec((1, ret_w)),
        ],
        out_specs=[
            pl.BlockSpec((1, tl, s5w), lambda b, i: (b, i, 0)),
            pl.BlockSpec((1, tl, ret_w), lambda b, i: (b, i, 0)),
        ],
        out_shape=[jax.ShapeDtypeStruct((bsz, seq, s5w), BF16),
                   jax.ShapeDtypeStruct((bsz, seq, ret_w), BF16)],
        scratch_shapes=[pltpu.VMEM((RET_HEADS // 2, 2 * LANES, 2 * RET_DV), F32)],
        compiler_params=pltpu.CompilerParams(
            dimension_semantics=("arbitrary", "arbitrary"), vmem_limit_bytes=VMEM_LIMIT_BYTES),
        name="mixer_front",
    )(x, w, cos, sin, hm, dmask, xi_p, zk, gdec_p, smask, gn_gain.reshape(1, ret_w).astype(F32))


def _same_group(shape, row_shift, col_shift):
    gpl_mask = LANES // S5_GROUP - 1
    rows = lax.broadcasted_iota(jnp.int32, shape, 0) >> row_shift
    cols = (lax.broadcasted_iota(jnp.int32, shape, 1) >> col_shift) & gpl_mask
    return (rows == cols).astype(F32)


def _s5_build(kt_ref, at_ref, ct_ref, t_ref, e_ref, c_ref):
    q = kt_ref.shape[1]
    gpl = LANES // S5_GROUP
    ch_shift = S5_GROUP.bit_length() - 1
    st_shift = S5_STATE.bit_length() - 1
    t_ref[...] = jnp.zeros_like(t_ref)
    mask_t = _same_group((LANES, LANES), ch_shift, ch_shift)
    for lag in range(q):
        bd = (jnp.concatenate([kt_ref[0, lag]] * gpl, axis=0) * mask_t).astype(BF16)
        for i in range(q - lag):
            j = i + lag
            t_ref[i * LANES:(i + 1) * LANES, j * LANES:(j + 1) * LANES] = bd
    mask_e = _same_group((LANES, e_ref.shape[1]), ch_shift, st_shift)
    for i in range(q):
        e_ref[i * LANES:(i + 1) * LANES, :] = (jnp.concatenate([at_ref[0, i]] * gpl, axis=0) * mask_e).astype(BF16)
    ns = c_ref.shape[0] // 2
    mask_c = _same_group((ns, c_ref.shape[1]), st_shift, ch_shift)
    for part in range(2):
        c_ref[part * ns:(part + 1) * ns, :] = (jnp.concatenate([ct_ref[0, part]] * gpl, axis=0) * mask_c).astype(BF16)


def _s5_kernel(u_ref, kt_ref, at_ref, ct_ref, aq_ref, y_ref, t_ref, e_ref, c_ref, sprev_ref, carry_ref):
    @pl.when(pl.program_id(1) == 0)
    def _():
        carry_ref[...] = jnp.zeros_like(carry_ref)
        _s5_build(kt_ref, at_ref, ct_ref, t_ref, e_ref, c_ref)

    nb, q, bsz, _ = u_ref.shape
    xcat = jnp.concatenate([u_ref[:, j].reshape(nb * bsz, LANES) for j in range(q)], axis=1)
    e = jnp.dot(xcat, e_ref[...], preferred_element_type=F32)
    ns = e.shape[1] // 2
    aqr = aq_ref[0, :, 0:ns]
    aqi = aq_ref[0, :, ns:2 * ns]
    s_re = carry_ref[:, 0:ns]
    s_im = carry_ref[:, ns:2 * ns]
    for n in range(nb):
        r0 = n * bsz
        sprev_ref[r0:r0 + bsz, :] = jnp.concatenate([s_re, s_im], axis=1).astype(BF16)
        e_re = e[r0:r0 + bsz, 0:ns]
        e_im = e[r0:r0 + bsz, ns:2 * ns]
        s_re, s_im = aqr * s_re - aqi * s_im + e_re, aqr * s_im + aqi * s_re + e_im
    carry_ref[...] = jnp.concatenate([s_re, s_im], axis=1)
    tw = 2 * LANES
    for jt in range(q * LANES // tw):
        hi = (jt + 1) * tw
        y = jnp.dot(xcat[:, 0:hi], t_ref[0:hi, jt * tw:hi], preferred_element_type=F32)
        y = y + jnp.dot(sprev_ref[...], c_ref[:, jt * tw:hi], preferred_element_type=F32)
        for j in range(jt * tw // LANES, hi // LANES):
            yj = y[:, j * LANES - jt * tw:(j + 1) * LANES - jt * tw]
            y_ref[:, j] = yj.reshape(nb, bsz, LANES).astype(BF16)


def _s5_matrices(lam_re, lam_im, b_re, b_im, c_re, c_im, d, log_step, q):
    g, p = lam_re.shape
    cch = b_re.shape[-1]
    gpl = LANES // cch
    nblk = g // gpl
    dt = jnp.exp(log_step)[:, None]
    zr, zi = lam_re * dt, lam_im * dt
    ks = jnp.arange(q + 1, dtype=F32)[:, None, None]
    mag = jnp.exp(ks * zr[None])
    pr, pi = mag * jnp.cos(ks * zi[None]), mag * jnp.sin(ks * zi[None])
    nr, ni = pr[1] - 1.0, pi[1]
    den = lam_re * lam_re + lam_im * lam_im
    fr = (nr * lam_re + ni * lam_im) / den
    fi = (ni * lam_re - nr * lam_im) / den
    bbr = fr[..., None] * b_re - fi[..., None] * b_im
    bbi = fr[..., None] * b_im + fi[..., None] * b_re
    wr = pr[:q, :, :, None] * bbr[None] - pi[:q, :, :, None] * bbi[None]
    wi = pr[:q, :, :, None] * bbi[None] + pi[:q, :, :, None] * bbr[None]
    kern = (jnp.einsum('gop,kgpc->kgoc', c_re, wr, precision=HI)
            - jnp.einsum('gop,kgpc->kgoc', c_im, wi, precision=HI))
    kern = kern.at[0].add(jnp.eye(cch, dtype=F32)[None] * d.reshape(g, 1, cch))
    kt = kern.reshape(q, nblk, gpl, cch, cch).transpose(1, 0, 4, 2, 3).reshape(nblk, q, cch, LANES)
    er = wr[::-1].reshape(q, nblk, gpl, p, cch).transpose(1, 0, 4, 2, 3)
    ei = wi[::-1].reshape(q, nblk, gpl, p, cch).transpose(1, 0, 4, 2, 3)
    at = jnp.concatenate([er.reshape(nblk, q, cch, gpl * p), ei.reshape(nblk, q, cch, gpl * p)], axis=3)
    cr = c_re[None] * pr[1:, :, None, :] - c_im[None] * pi[1:, :, None, :]
    ci = c_re[None] * pi[1:, :, None, :] + c_im[None] * pr[1:, :, None, :]
    cr = cr.reshape(q, nblk, gpl, cch, p).transpose(1, 4, 0, 2, 3).reshape(nblk, p, q * LANES)
    ci = ci.reshape(q, nblk, gpl, cch, p).transpose(1, 4, 0, 2, 3).reshape(nblk, p, q * LANES)
    ct = jnp.stack([cr, -ci], axis=1)
    aq = jnp.concatenate([pr[q].reshape(nblk, gpl * p), pi[q].reshape(nblk, gpl * p)], axis=1)
    return kt, at, ct, aq


def _s5_scan(u, mats):
    kt, at, ct, aq = mats
    bsz, seq, w = u.shape
    q, nb = S5_Q, S5_NB
    nblk = w // LANES
    ns2 = aq.shape[-1]
    u4 = jnp.transpose(u, (1, 0, 2)).reshape(seq // q, q, bsz, w)
    aq_b = jnp.broadcast_to(aq[:, None, :], (nblk, bsz, ns2))
    blk = pl.BlockSpec((nb, q, bsz, LANES), lambda m, i: (i, 0, 0, m))
    tab = lambda a: pl.BlockSpec((1,) + a.shape[1:], lambda m, i: (m,) + (0,) * (a.ndim - 1))
    y4 = pl.pallas_call(
        _s5_kernel,
        grid=(nblk, seq // (q * nb)),
        in_specs=[blk, tab(kt), tab(at), tab(ct), tab(aq_b)],
        out_specs=blk,
        out_shape=jax.ShapeDtypeStruct(u4.shape, BF16),
        scratch_shapes=[pltpu.VMEM((q * LANES, q * LANES), BF16),
                        pltpu.VMEM((q * LANES, ns2), BF16),
                        pltpu.VMEM((ns2, q * LANES), BF16),
                        pltpu.VMEM((nb * bsz, ns2), BF16),
                        pltpu.VMEM((bsz, ns2), F32)],
        compiler_params=pltpu.CompilerParams(
            dimension_semantics=("arbitrary", "arbitrary"), vmem_limit_bytes=VMEM_LIMIT_BYTES),
        name="s5_scan",
    )(u4, kt, at, ct, aq_b)
    return jnp.transpose(y4.reshape(seq, bsz, w), (1, 0, 2))


def _ffn_kernel(alpha, x_ref, ys_ref, yr_ref, gluw_ref, glub_ref, s5g_ref, wout_ref, ln1g_ref, ln1b_ref,
                wup_ref, cw_ref, cb_ref, wdn_ref, ln2g_ref, ln2b_ref, o_ref, hmid_ref, tail_ref):
    @pl.when(pl.program_id(1) == 0)
    def _():
        tail_ref[...] = jnp.zeros_like(tail_ref)

    tm = x_ref.shape[1]
    ts = FFN_SUB
    n_sub = tm // ts
    s5w = ys_ref.shape[-1]
    dff = hmid_ref.shape[1]
    halo = tail_ref.shape[0]

    def mix(r0):
        y = ys_ref[0, r0:r0 + ts, :].astype(F32)
        cdf = 0.5 * (1.0 + jnp.tanh(math.sqrt(2.0 / math.pi) * (y + 0.044715 * (y * y * y))))
        y = y * cdf
        z = jnp.dot(y.astype(BF16), gluw_ref[...], preferred_element_type=F32) + glub_ref[...]
        y = y * _sigmoid(z)
        y = y * lax.rsqrt(jnp.mean(y * y, axis=-1, keepdims=True) + LN_EPS) * s5g_ref[...]
        mix = jnp.dot(y.astype(BF16), wout_ref[0:s5w, :], preferred_element_type=F32)
        mix = mix + jnp.dot(yr_ref[0, r0:r0 + ts, :], wout_ref[s5w:, :], preferred_element_type=F32)
        return _layer_norm(alpha * x_ref[0, r0:r0 + ts, :] + mix, ln1g_ref[...], ln1b_ref[...])

    def chunk(r0, xb, c):
        c0 = c * FC_FFN
        a = jnp.dot(xb, wup_ref[:, c0:c0 + FC_FFN], preferred_element_type=F32)
        g = jnp.dot(xb, wup_ref[:, dff + c0:dff + c0 + FC_FFN], preferred_element_type=F32)
        ext = jnp.concatenate([tail_ref[:, c0:c0 + FC_FFN], a], axis=0)
        tail_ref[:, c0:c0 + FC_FFN] = a[ts - halo:ts]
        a1 = ext[halo - 1:halo - 1 + ts]
        a2 = ext[halo - 2:halo - 2 + ts]
        cv = (cw_ref[2:3, c0:c0 + FC_FFN] * a + cw_ref[1:2, c0:c0 + FC_FFN] * a1
              + cw_ref[0:1, c0:c0 + FC_FFN] * a2 + cb_ref[:, c0:c0 + FC_FFN])
        hmid_ref[r0:r0 + ts, c0:c0 + FC_FFN] = (cv * _sigmoid(cv) * g).astype(BF16)

    def out(r0, res):
        o_ref[0, r0:r0 + ts, :] = _layer_norm(res, ln2g_ref[...], ln2b_ref[...])

    x1 = mix(0)
    res_prev = None
    for sub in range(n_sub):
        r0 = sub * ts
        xb = x1.astype(BF16)
        x1_next = None
        for c in range(dff // FC_FFN):
            chunk(r0, xb, c)
            if c == FFN_MIX_AT and sub + 1 < n_sub:
                x1_next = mix(r0 + ts)
            if c == FFN_OUT_AT and res_prev is not None:
                out(r0 - ts, res_prev)
        res_prev = alpha * x1 + jnp.dot(hmid_ref[r0:r0 + ts, :], wdn_ref[...], preferred_element_type=F32)
        x1 = x1_next
    out((n_sub - 1) * ts, res_prev)


def _mixer_ffn(alpha, x, ys, yr, glu_w, glu_b, s5_gain, w_out, ln1_g, ln1_b,
               w_up, conv_w, conv_b, w_down, ln2_g, ln2_b):
    bsz, seq, d = x.shape
    tm = TM_FFN
    s5w, ret_w = ys.shape[-1], yr.shape[-1]
    dff = w_down.shape[0]
    row = lambda a: a.reshape(1, -1).astype(F32)
    tok = lambda w: pl.BlockSpec((1, tm, w), lambda b, i: (b, i, 0))
    operands = [x, ys, yr, glu_w.astype(BF16), row(glu_b), row(s5_gain), w_out.astype(BF16),
                row(ln1_g), row(ln1_b), w_up.astype(BF16), conv_w.astype(F32), row(conv_b),
                w_down.astype(BF16), row(ln2_g), row(ln2_b)]
    in_specs = [tok(d), tok(s5w), tok(ret_w)] + [_const_spec(a.shape) for a in operands[3:]]
    return pl.pallas_call(
        functools.partial(_ffn_kernel, alpha),
        grid=(bsz, seq // tm),
        in_specs=in_specs,
        out_specs=tok(d),
        out_shape=jax.ShapeDtypeStruct((bsz, seq, d), x.dtype),
        scratch_shapes=[pltpu.VMEM((tm, dff), BF16), pltpu.VMEM((8, dff), F32)],
        compiler_params=pltpu.CompilerParams(
            dimension_semantics=("arbitrary", "arbitrary"), vmem_limit_bytes=VMEM_LIMIT_BYTES),
        name="mixer_ffn",
    )(*operands)


def kernel(x, w_in, s5_lambda_re, s5_lambda_im, s5_b_re, s5_b_im, s5_c_re, s5_c_im, s5_d, s5_log_step,
           s5_glu_w, s5_glu_b, s5_out_gain, ret_gn_gain, w_out, ln1_g, ln1_b, ffn_w_up, ffn_conv_w,
           ffn_conv_b, ffn_w_down, ln2_g, ln2_b):
    depth = w_in.shape[0]
    alpha = (2.0 * depth) ** 0.25
    s5w = s5_d.shape[-1]
    for l in range(depth):
        u, yr = _mixer_front(x, w_in[l], ret_gn_gain[l], s5w)
        mats = _s5_matrices(s5_lambda_re[l], s5_lambda_im[l], s5_b_re[l], s5_b_im[l], s5_c_re[l],
                            s5_c_im[l], s5_d[l], s5_log_step[l], S5_Q)
        ys = _s5_scan(u, mats)
        x = _mixer_ffn(alpha, x, ys, yr, s5_glu_w[l], s5_glu_b[l], s5_out_gain[l], w_out[l], ln1_g[l],
                       ln1_b[l], ffn_w_up[l], ffn_conv_w[l], ffn_conv_b[l], ffn_w_down[l], ln2_g[l], ln2_b[l])
    return x
```

```python
import functools
import math

import numpy as np
import jax
import jax.numpy as jnp
from jax import lax
from jax.experimental import pallas as pl
from jax.experimental.pallas import tpu as pltpu

CHUNK = 64
S5_GROUP = 16
S5_STATE = 64
RET_HEADS = 4
RET_DK = 64
RET_DV = 128
CONV_W = 3
ROPE_BASE = 10000.0
LN_EPS = 1e-5

LANES = 128
SUBLANES_BF16 = 16
VMEM_LIMIT_BYTES = 56 * 1024 * 1024

TL_FRONT = 1024
TB_RET = 256
S5_Q = 8
S5_NB = 64
TM_FFN = 512
FC_FFN = 256
FFN_SUB = 256
FFN_MIX_AT = 1
FFN_OUT_AT = 6

F32 = jnp.float32
BF16 = jnp.bfloat16
HI = lax.Precision.HIGHEST


def _const_spec(shape):
    nd = len(shape)
    return pl.BlockSpec(shape, lambda *_: (0,) * nd, pipeline_mode=pl.Buffered(1))


def _layer_norm(v, g, b):
    mu = jnp.mean(v, axis=-1, keepdims=True)
    vc = v - mu
    var = jnp.mean(vc * vc, axis=-1, keepdims=True)
    return vc * lax.rsqrt(var + LN_EPS) * g + b


def _sigmoid(v):
    return 1.0 / (1.0 + jnp.exp(-v))


def _front_kernel(x_ref, w_ref, cos_ref, sin_ref, hm_ref, dmask_ref, xi_ref, zk_ref,
                  gdec_ref, smask_ref, gn_ref, u_ref, yr_ref, state_ref):
    @pl.when(pl.program_id(1) == 0)
    def _():
        state_ref[...] = jnp.zeros_like(state_ref)

    s5w = u_ref.shape[-1]
    xb = x_ref[0].astype(BF16)
    o_qk = s5w
    o_v = o_qk + 4 * LANES
    ret_w = RET_HEADS * RET_DV
    pw = 2 * RET_DV

    def project(r0):
        xr = xb[r0:r0 + TB_RET]
        qk = jnp.dot(xr, w_ref[:, o_qk:o_qk + 4 * LANES], preferred_element_type=F32)
        cos = cos_ref[r0:r0 + TB_RET, :]
        sin = sin_ref[r0:r0 + TB_RET, :]
        q1, q2 = qk[:, 0:LANES], qk[:, LANES:2 * LANES]
        k1, k2 = qk[:, 2 * LANES:3 * LANES], qk[:, 3 * LANES:4 * LANES]
        qb = jnp.concatenate([q1 * cos - q2 * sin, q1 * sin + q2 * cos], axis=1).astype(BF16)
        kb = jnp.concatenate([k1 * cos - k2 * sin, k1 * sin + k2 * cos], axis=1)
        v = jnp.dot(xr, w_ref[:, o_v:o_v + ret_w], preferred_element_type=F32).astype(BF16)
        gate = jnp.dot(xr, w_ref[:, o_v + ret_w:o_v + 2 * ret_w], preferred_element_type=F32)
        return qb, kb, v, gate

    def retain(r0, qb, kb, v, gate):
        kbb = kb.astype(BF16)
        for pair in range(RET_HEADS // 2):
            p0 = pair * pw
            st = state_ref[pair]
            cross = xi_ref[pair] * jnp.dot(qb, st.astype(BF16), preferred_element_type=F32)
            kz = (kb * zk_ref[pair]).astype(BF16)
            upd = lax.dot_general(kz, v[:, p0:p0 + pw], (((0,), (0,)), ((), ())), preferred_element_type=F32)
            state_ref[pair] = st * gdec_ref[pair] + upd * smask_ref[pair]
            for sub in range(2):
                hd = 2 * pair + sub
                c0 = hd * RET_DV
                kh = kbb * hm_ref[hd]
                sc = lax.dot_general(qb, kh, (((1,), (1,)), ((), ())), preferred_element_type=F32)
                p = (sc * dmask_ref[hd]).astype(BF16)
                o = jnp.dot(p, v[:, c0:c0 + RET_DV], preferred_element_type=F32)
                o = o + cross[:, sub * RET_DV:(sub + 1) * RET_DV]
                mu = jnp.mean(o, axis=-1, keepdims=True)
                oc = o - mu
                var = jnp.mean(oc * oc, axis=-1, keepdims=True)
                on = oc * lax.rsqrt(var + LN_EPS) * gn_ref[:, c0:c0 + RET_DV]
                g = gate[:, c0:c0 + RET_DV]
                yr_ref[0, r0:r0 + TB_RET, c0:c0 + RET_DV] = (on * (g * _sigmoid(g))).astype(BF16)

    n_blk = xb.shape[0] // TB_RET
    cur = project(0)
    for blk in range(n_blk):
        nxt = project((blk + 1) * TB_RET) if blk + 1 < n_blk else None
        retain(blk * TB_RET, *cur)
        cur = nxt

    u_ref[0] = jnp.dot(xb, w_ref[:, 0:s5w], preferred_element_type=F32).astype(BF16)


def _retention_consts(tb):
    log_gamma = np.log1p(-(2.0 ** (-5.0 - np.arange(RET_HEADS, dtype=np.float64))))
    pos = np.arange(tb, dtype=np.float64)
    chunk = np.arange(tb) // CHUNK
    visible = (chunk[None, :] <= chunk[:, None]).astype(np.float64)
    dist = np.abs(pos[:, None] - pos[None, :])
    dmask = np.exp(dist[None] * log_gamma[:, None, None]) * visible[None]
    xi = np.exp((pos + 1.0)[None, :] * log_gamma[:, None])
    zeta = np.exp((tb - 1.0 - pos)[None, :] * log_gamma[:, None])
    gdec = np.exp(tb * log_gamma)
    lane = np.arange(2 * LANES) % LANES
    half = RET_DK // 2
    hm = np.stack([(lane // half == h) for h in range(RET_HEADS)]).astype(np.float64) * RET_DK ** -0.5
    npair = RET_HEADS // 2
    own = (hm > 0).astype(np.float64)
    xi_p = np.repeat(xi[:, :, None], RET_DV, axis=2).reshape(npair, 2, tb, RET_DV)
    xi_p = np.concatenate([xi_p[:, 0], xi_p[:, 1]], axis=2)
    zk = (zeta[:, :, None] * hm[:, None, :]).reshape(npair, 2, tb, 2 * LANES).sum(axis=1)
    col_head = np.repeat(np.arange(RET_HEADS).reshape(npair, 2), RET_DV, axis=1)
    smask = np.stack([own[col_head[p]].T for p in range(npair)])
    gdec_p = smask * gdec[col_head][:, None, :]
    tables = tuple(jnp.asarray(np.ascontiguousarray(a), F32) for a in (dmask, xi_p, zk, gdec_p, smask))
    return (jnp.asarray(hm[:, None, :], BF16),) + tables


def _front_weight(w_in, s5w):
    half = RET_DK // 2
    nqk = RET_HEADS * RET_DK
    w = w_in.astype(BF16)

    def regroup(cols):
        return cols.reshape(-1, RET_HEADS, 2, half).transpose(0, 2, 1, 3).reshape(-1, nqk)

    return jnp.concatenate([w[:, :s5w], regroup(w[:, s5w:s5w + nqk]),
                            regroup(w[:, s5w + nqk:s5w + 2 * nqk]), w[:, s5w + 2 * nqk:]], axis=1)


def _mixer_front(x, w_in, gn_gain, s5w):
    bsz, seq, d = x.shape
    tl = TL_FRONT
    ret_w = RET_HEADS * RET_DV
    half = RET_DK // 2
    freqs = np.float32(ROPE_BASE) ** (-np.arange(half, dtype=np.float32) / np.float32(half))
    ang = (np.arange(seq, dtype=np.float32)[:, None] * freqs[None, :]).astype(np.float64)
    cos = jnp.asarray(np.tile(np.cos(ang), (1, RET_HEADS)), F32)
    sin = jnp.asarray(np.tile(np.sin(ang), (1, RET_HEADS)), F32)
    hm, dmask, xi_p, zk, gdec_p, smask = _retention_consts(TB_RET)
    w = _front_weight(w_in, s5w)
    return pl.pallas_call(
        _front_kernel,
        grid=(bsz, seq // tl),
        in_specs=[
            pl.BlockSpec((1, tl, d), lambda b, i: (b, i, 0)),
            _const_spec(w.shape),
            pl.BlockSpec((tl, LANES), lambda b, i: (i, 0)),
            pl.BlockSpec((tl, LANES), lambda b, i: (i, 0)),
            _const_spec(hm.shape), _const_spec(dmask.shape), _const_spec(xi_p.shape),
            _const_spec(zk.shape), _const_spec(gdec_p.shape), _const_spec(smask.shape), _const_spec((1, ret_w)),
        ],
        out_specs=[
            pl.BlockSpec((1, tl, s5w), lambda b, i: (b, i, 0)),
            pl.BlockSpec((1, tl, ret_w), lambda b, i: (b, i, 0)),
        ],
        out_shape=[jax.ShapeDtypeStruct((bsz, seq, s5w), BF16),
                   jax.ShapeDtypeStruct((bsz, seq, ret_w), BF16)],
        scratch_shapes=[pltpu.VMEM((RET_HEADS // 2, 2 * LANES, 2 * RET_DV), F32)],
        compiler_params=pltpu.CompilerParams(
            dimension_semantics=("arbitrary", "arbitrary"), vmem_limit_bytes=VMEM_LIMIT_BYTES),
        name="mixer_front",
    )(x, w, cos, sin, hm, dmask, xi_p, zk, gdec_p, smask, gn_gain.reshape(1, ret_w).astype(F32))


def _same_group(shape, row_shift, col_shift):
    gpl_mask = LANES // S5_GROUP - 1
    rows = lax.broadcasted_iota(jnp.int32, shape, 0) >> row_shift
    cols = (lax.broadcasted_iota(jnp.int32, shape, 1) >> col_shift) & gpl_mask
    return (rows == cols).astype(F32)


def _s5_build(kt_ref, at_ref, ct_ref, t_ref, e_ref, c_ref):
    q = kt_ref.shape[1]
    gpl = LANES // S5_GROUP
    ch_shift = S5_GROUP.bit_length() - 1
    st_shift = S5_STATE.bit_length() - 1
    t_ref[...] = jnp.zeros_like(t_ref)
    mask_t = _same_group((LANES, LANES), ch_shift, ch_shift)
    for lag in range(q):
        bd = (jnp.concatenate([kt_ref[0, lag]] * gpl, axis=0) * mask_t).astype(BF16)
        for i in range(q - lag):
            j = i + lag
            t_ref[i * LANES:(i + 1) * LANES, j * LANES:(j + 1) * LANES] = bd
    mask_e = _same_group((LANES, e_ref.shape[1]), ch_shift, st_shift)
    for i in range(q):
        e_ref[i * LANES:(i + 1) * LANES, :] = (jnp.concatenate([at_ref[0, i]] * gpl, axis=0) * mask_e).astype(BF16)
    ns = c_ref.shape[0] // 2
    mask_c = _same_group((ns, c_ref.shape[1]), st_shift, ch_shift)
    for part in range(2):
        c_ref[part * ns:(part + 1) * ns, :] = (jnp.concatenate([ct_ref[0, part]] * gpl, axis=0) * mask_c).astype(BF16)


def _s5_kernel(u_ref, kt_ref, at_ref, ct_ref, aq_ref, y_ref, t_ref, e_ref, c_ref, sprev_ref, carry_ref):
    @pl.when(pl.program_id(1) == 0)
    def _():
        carry_ref[...] = jnp.zeros_like(carry_ref)
        _s5_build(kt_ref, at_ref, ct_ref, t_ref, e_ref, c_ref)

    nb, q, bsz, _ = u_ref.shape
    xcat = jnp.concatenate([u_ref[:, j].reshape(nb * bsz, LANES) for j in range(q)], axis=1)
    e = jnp.dot(xcat, e_ref[...], preferred_element_type=F32)
    ns = e.shape[1] // 2
    aqr = aq_ref[0, :, 0:ns]
    aqi = aq_ref[0, :, ns:2 * ns]
    s_re = carry_ref[:, 0:ns]
    s_im = carry_ref[:, ns:2 * ns]
    for n in range(nb):
        r0 = n * bsz
        sprev_ref[r0:r0 + bsz, :] = jnp.concatenate([s_re, s_im], axis=1).astype(BF16)
        e_re = e[r0:r0 + bsz, 0:ns]
        e_im = e[r0:r0 + bsz, ns:2 * ns]
        s_re, s_im = aqr * s_re - aqi * s_im + e_re, aqr * s_im + aqi * s_re + e_im
    carry_ref[...] = jnp.concatenate([s_re, s_im], axis=1)
    tw = 2 * LANES
    for jt in range(q * LANES // tw):
        hi = (jt + 1) * tw
        y = jnp.dot(xcat[:, 0:hi], t_ref[0:hi, jt * tw:hi], preferred_element_type=F32)
        y = y + jnp.dot(sprev_ref[...], c_ref[:, jt * tw:hi], preferred_element_type=F32)
        for j in range(jt * tw // LANES, hi // LANES):
            yj = y[:, j * LANES - jt * tw:(j + 1) * LANES - jt * tw]
            y_ref[:, j] = yj.reshape(nb, bsz, LANES).astype(BF16)


def _s5_matrices(lam_re, lam_im, b_re, b_im, c_re, c_im, d, log_step, q):
    g, p = lam_re.shape
    cch = b_re.shape[-1]
    gpl = LANES // cch
    nblk = g // gpl
    dt = jnp.exp(log_step)[:, None]
    zr, zi = lam_re * dt, lam_im * dt
    ks = jnp.arange(q + 1, dtype=F32)[:, None, None]
    mag = jnp.exp(ks * zr[None])
    pr, pi = mag * jnp.cos(ks * zi[None]), mag * jnp.sin(ks * zi[None])
    nr, ni = pr[1] - 1.0, pi[1]
    den = lam_re * lam_re + lam_im * lam_im
    fr = (nr * lam_re + ni * lam_im) / den
    fi = (ni * lam_re - nr * lam_im) / den
    bbr = fr[..., None] * b_re - fi[..., None] * b_im
    bbi = fr[..., None] * b_im + fi[..., None] * b_re
    wr = pr[:q, :, :, None] * bbr[None] - pi[:q, :, :, None] * bbi[None]
    wi = pr[:q, :, :, None] * bbi[None] + pi[:q, :, :, None] * bbr[None]
    kern = (jnp.einsum('gop,kgpc->kgoc', c_re, wr, precision=HI)
            - jnp.einsum('gop,kgpc->kgoc', c_im, wi, precision=HI))
    kern = kern.at[0].add(jnp.eye(cch, dtype=F32)[None] * d.reshape(g, 1, cch))
    kt = kern.reshape(q, nblk, gpl, cch, cch).transpose(1, 0, 4, 2, 3).reshape(nblk, q, cch, LANES)
    er = wr[::-1].reshape(q, nblk, gpl, p, cch).transpose(1, 0, 4, 2, 3)
    ei = wi[::-1].reshape(q, nblk, gpl, p, cch).transpose(1, 0, 4, 2, 3)
    at = jnp.concatenate([er.reshape(nblk, q, cch, gpl * p), ei.reshape(nblk, q, cch, gpl * p)], axis=3)
    cr = c_re[None] * pr[1:, :, None, :] - c_im[None] * pi[1:, :, None, :]
    ci = c_re[None] * pi[1:, :, None, :] + c_im[None] * pr[1:, :, None, :]
    cr = cr.reshape(q, nblk, gpl, cch, p).transpose(1, 4, 0, 2, 3).reshape(nblk, p, q * LANES)
    ci = ci.reshape(q, nblk, gpl, cch, p).transpose(1, 4, 0, 2, 3).reshape(nblk, p, q * LANES)
    ct = jnp.stack([cr, -ci], axis=1)
    aq = jnp.concatenate([pr[q].reshape(nblk, gpl * p), pi[q].reshape(nblk, gpl * p)], axis=1)
    return kt, at, ct, aq


def _s5_scan(u, mats):
    kt, at, ct, aq = mats
    bsz, seq, w = u.shape
    q, nb = S5_Q, S5_NB
    nblk = w // LANES
    ns2 = aq.shape[-1]
    u4 = jnp.transpose(u, (1, 0, 2)).reshape(seq // q, q, bsz, w)
    aq_b = jnp.broadcast_to(aq[:, None, :], (nblk, bsz, ns2))
    blk = pl.BlockSpec((nb, q, bsz, LANES), lambda m, i: (i, 0, 0, m))
    tab = lambda a: pl.BlockSpec((1,) + a.shape[1:], lambda m, i: (m,) + (0,) * (a.ndim - 1))
    y4 = pl.pallas_call(
        _s5_kernel,
        grid=(nblk, seq // (q * nb)),
        in_specs=[blk, tab(kt), tab(at), tab(ct), tab(aq_b)],
        out_specs=blk,
        out_shape=jax.ShapeDtypeStruct(u4.shape, BF16),
        scratch_shapes=[pltpu.VMEM((q * LANES, q * LANES), BF16),
                        pltpu.VMEM((q * LANES, ns2), BF16),
                        pltpu.VMEM((ns2, q * LANES), BF16),
                        pltpu.VMEM((nb * bsz, ns2), BF16),
                        pltpu.VMEM((bsz, ns2), F32)],
        compiler_params=pltpu.CompilerParams(
            dimension_semantics=("arbitrary", "arbitrary"), vmem_limit_bytes=VMEM_LIMIT_BYTES),
        name="s5_scan",
    )(u4, kt, at, ct, aq_b)
    return jnp.transpose(y4.reshape(seq, bsz, w), (1, 0, 2))


def _ffn_kernel(alpha, x_ref, ys_ref, yr_ref, gluw_ref, glub_ref, s5g_ref, wout_ref, ln1g_ref, ln1b_ref,
                wup_ref, cw_ref, cb_ref, wdn_ref, ln2g_ref, ln2b_ref, o_ref, hmid_ref, tail_ref):
    @pl.when(pl.program_id(1) == 0)
    def _():
        tail_ref[...] = jnp.zeros_like(tail_ref)

    tm = x_ref.shape[1]
    ts = FFN_SUB
    n_sub = tm // ts
    s5w = ys_ref.shape[-1]
    dff = hmid_ref.shape[1]
    halo = tail_ref.shape[0]

    def mix(r0):
        y = ys_ref[0, r0:r0 + ts, :].astype(F32)
        cdf = 0.5 * (1.0 + jnp.tanh(math.sqrt(2.0 / math.pi) * (y + 0.044715 * (y * y * y))))
        y = y * cdf
        z = jnp.dot(y.astype(BF16), gluw_ref[...], preferred_element_type=F32) + glub_ref[...]
        y = y * _sigmoid(z)
        y = y * lax.rsqrt(jnp.mean(y * y, axis=-1, keepdims=True) + LN_EPS) * s5g_ref[...]
        mix = jnp.dot(y.astype(BF16), wout_ref[0:s5w, :], preferred_element_type=F32)
        mix = mix + jnp.dot(yr_ref[0, r0:r0 + ts, :], wout_ref[s5w:, :], preferred_element_type=F32)
        return _layer_norm(alpha * x_ref[0, r0:r0 + ts, :] + mix, ln1g_ref[...], ln1b_ref[...])

    def chunk(r0, xb, c):
        c0 = c * FC_FFN
        a = jnp.dot(xb, wup_ref[:, c0:c0 + FC_FFN], preferred_element_type=F32)
        g = jnp.dot(xb, wup_ref[:, dff + c0:dff + c0 + FC_FFN], preferred_element_type=F32)
        ext = jnp.concatenate([tail_ref[:, c0:c0 + FC_FFN], a], axis=0)
        tail_ref[:, c0:c0 + FC_FFN] = a[ts - halo:ts]
        a1 = ext[halo - 1:halo - 1 + ts]
        a2 = ext[halo - 2:halo - 2 + ts]
        cv = (cw_ref[2:3, c0:c0 + FC_FFN] * a + cw_ref[1:2, c0:c0 + FC_FFN] * a1
              + cw_ref[0:1, c0:c0 + FC_FFN] * a2 + cb_ref[:, c0:c0 + FC_FFN])
        hmid_ref[r0:r0 + ts, c0:c0 + FC_FFN] = (cv * _sigmoid(cv) * g).astype(BF16)

    def out(r0, res):
        o_ref[0, r0:r0 + ts, :] = _layer_norm(res, ln2g_ref[...], ln2b_ref[...])

    x1 = mix(0)
    res_prev = None
    for sub in range(n_sub):
        r0 = sub * ts
        xb = x1.astype(BF16)
        x1_next = None
        for c in range(dff // FC_FFN):
            chunk(r0, xb, c)
            if c == FFN_MIX_AT and sub + 1 < n_sub:
                x1_next = mix(r0 + ts)
            if c == FFN_OUT_AT and res_prev is not None:
                out(r0 - ts, res_prev)
        res_prev = alpha * x1 + jnp.dot(hmid_ref[r0:r0 + ts, :], wdn_ref[...], preferred_element_type=F32)
        x1 = x1_next
    out((n_sub - 1) * ts, res_prev)


def _mixer_ffn(alpha, x, ys, yr, glu_w, glu_b, s5_gain, w_out, ln1_g, ln1_b,
               w_up, conv_w, conv_b, w_down, ln2_g, ln2_b):
    bsz, seq, d = x.shape
    tm = TM_FFN
    s5w, ret_w = ys.shape[-1], yr.shape[-1]
    dff = w_down.shape[0]
    row = lambda a: a.reshape(1, -1).astype(F32)
    tok = lambda w: pl.BlockSpec((1, tm, w), lambda b, i: (b, i, 0))
    operands = [x, ys, yr, glu_w.astype(BF16), row(glu_b), row(s5_gain), w_out.astype(BF16),
                row(ln1_g), row(ln1_b), w_up.astype(BF16), conv_w.astype(F32), row(conv_b),
                w_down.astype(BF16), row(ln2_g), row(ln2_b)]
    in_specs = [tok(d), tok(s5w), tok(ret_w)] + [_const_spec(a.shape) for a in operands[3:]]
    return pl.pallas_call(
        functools.partial(_ffn_kernel, alpha),
        grid=(bsz, seq // tm),
        in_specs=in_specs,
        out_specs=tok(d),
        out_shape=jax.ShapeDtypeStruct((bsz, seq, d), x.dtype),
        scratch_shapes=[pltpu.VMEM((tm, dff), BF16), pltpu.VMEM((8, dff), F32)],
        compiler_params=pltpu.CompilerParams(
            dimension_semantics=("arbitrary", "arbitrary"), vmem_limit_bytes=VMEM_LIMIT_BYTES),
        name="mixer_ffn",
    )(*operands)


def kernel(x, w_in, s5_lambda_re, s5_lambda_im, s5_b_re, s5_b_im, s5_c_re, s5_c_im, s5_d, s5_log_step,
           s5_glu_w, s5_glu_b, s5_out_gain, ret_gn_gain, w_out, ln1_g, ln1_b, ffn_w_up, ffn_conv_w,
           ffn_conv_b, ffn_w_down, ln2_g, ln2_b):
    depth = w_in.shape[0]
    alpha = (2.0 * depth) ** 0.25
    s5w = s5_d.shape[-1]
    for l in range(depth):
        u, yr = _mixer_front(x, w_in[l], ret_gn_gain[l], s5w)
        mats = _s5_matrices(s5_lambda_re[l], s5_lambda_im[l], s5_b_re[l], s5_b_im[l], s5_c_re[l],
                            s5_c_im[l], s5_d[l], s5_log_step[l], S5_Q)
        ys = _s5_scan(u, mats)
        x = _mixer_ffn(alpha, x, ys, yr, s5_glu_w[l], s5_glu_b[l], s5_out_gain[l], w_out[l], ln1_g[l],
                       ln1_b[l], ffn_w_up[l], ffn_conv_w[l], ffn_conv_b[l], ffn_w_down[l], ln2_g[l], ln2_b[l])
    return x
```

```python
import functools
import math

import numpy as np
import jax
import jax.numpy as jnp
from jax import lax
from jax.experimental import pallas as pl
from jax.experimental.pallas import tpu as pltpu

CHUNK = 64
S5_GROUP = 16
S5_STATE = 64
RET_HEADS = 4
RET_DK = 64
RET_DV = 128
CONV_W = 3
ROPE_BASE = 10000.0
LN_EPS = 1e-5

LANES = 128
SUBLANES_BF16 = 16
VMEM_LIMIT_BYTES = 56 * 1024 * 1024

TL_FRONT = 2048
TB_RET = 256
S5_Q = 8
S5_NB = 64
TM_FFN = 512
FC_FFN = 256
FFN_SUBS = (256, 256)
FFN_MIX_AT = 1
FFN_OUT_AT = 6

F32 = jnp.float32
BF16 = jnp.bfloat16
HI = lax.Precision.HIGHEST


def _const_spec(shape):
    nd = len(shape)
    return pl.BlockSpec(shape, lambda *_: (0,) * nd, pipeline_mode=pl.Buffered(1))


def _layer_norm(v, g, b):
    mu = jnp.mean(v, axis=-1, keepdims=True)
    vc = v - mu
    var = jnp.mean(vc * vc, axis=-1, keepdims=True)
    return vc * lax.rsqrt(var + LN_EPS) * g + b


def _sigmoid(v):
    return 1.0 / (1.0 + jnp.exp(-v))


def _front_kernel(x_ref, w_ref, cos_ref, sin_ref, hm_ref, dmask_ref, xi_ref, zk_ref,
                  gdec_ref, smask_ref, gn_ref, u_ref, yr_ref, state_ref):
    @pl.when(pl.program_id(1) == 0)
    def _():
        state_ref[...] = jnp.zeros_like(state_ref)

    s5w = u_ref.shape[-1]
    xb = x_ref[0].astype(BF16)
    o_qk = s5w
    o_v = o_qk + 4 * LANES
    ret_w = RET_HEADS * RET_DV
    pw = 2 * RET_DV

    def project(r0):
        xr = xb[r0:r0 + TB_RET]
        qk = jnp.dot(xr, w_ref[:, o_qk:o_qk + 4 * LANES], preferred_element_type=F32)
        cos = cos_ref[r0:r0 + TB_RET, :]
        sin = sin_ref[r0:r0 + TB_RET, :]
        q1, q2 = qk[:, 0:LANES], qk[:, LANES:2 * LANES]
        k1, k2 = qk[:, 2 * LANES:3 * LANES], qk[:, 3 * LANES:4 * LANES]
        qb = jnp.concatenate([q1 * cos - q2 * sin, q1 * sin + q2 * cos], axis=1).astype(BF16)
        kb = jnp.concatenate([k1 * cos - k2 * sin, k1 * sin + k2 * cos], axis=1)
        v = jnp.dot(xr, w_ref[:, o_v:o_v + ret_w], preferred_element_type=F32).astype(BF16)
        gate = jnp.dot(xr, w_ref[:, o_v + ret_w:o_v + 2 * ret_w], preferred_element_type=F32)
        return qb, kb, v, gate

    def retain(r0, qb, kb, v, gate):
        kbb = kb.astype(BF16)
        for pair in range(RET_HEADS // 2):
            p0 = pair * pw
            st = state_ref[pair]
            cross = xi_ref[pair] * jnp.dot(qb, st.astype(BF16), preferred_element_type=F32)
            kz = (kb * zk_ref[pair]).astype(BF16)
            upd = lax.dot_general(kz, v[:, p0:p0 + pw], (((0,), (0,)), ((), ())), preferred_element_type=F32)
            state_ref[pair] = st * gdec_ref[pair] + upd * smask_ref[pair]
            for sub in range(2):
                hd = 2 * pair + sub
                c0 = hd * RET_DV
                kh = kbb * hm_ref[hd]
                sc = lax.dot_general(qb, kh, (((1,), (1,)), ((), ())), preferred_element_type=F32)
                p = (sc * dmask_ref[hd]).astype(BF16)
                o = jnp.dot(p, v[:, c0:c0 + RET_DV], preferred_element_type=F32)
                o = o + cross[:, sub * RET_DV:(sub + 1) * RET_DV]
                mu = jnp.mean(o, axis=-1, keepdims=True)
                oc = o - mu
                var = jnp.mean(oc * oc, axis=-1, keepdims=True)
                on = oc * lax.rsqrt(var + LN_EPS) * gn_ref[:, c0:c0 + RET_DV]
                g = gate[:, c0:c0 + RET_DV]
                yr_ref[0, r0:r0 + TB_RET, c0:c0 + RET_DV] = (on * (g * _sigmoid(g))).astype(BF16)

    n_blk = xb.shape[0] // TB_RET
    cur = project(0)
    for blk in range(n_blk):
        nxt = project((blk + 1) * TB_RET) if blk + 1 < n_blk else None
        retain(blk * TB_RET, *cur)
        cur = nxt

    u_ref[0] = jnp.dot(xb, w_ref[:, 0:s5w], preferred_element_type=F32).astype(BF16)


def _retention_consts(tb):
    log_gamma = np.log1p(-(2.0 ** (-5.0 - np.arange(RET_HEADS, dtype=np.float64))))
    pos = np.arange(tb, dtype=np.float64)
    chunk = np.arange(tb) // CHUNK
    visible = (chunk[None, :] <= chunk[:, None]).astype(np.float64)
    dist = np.abs(pos[:, None] - pos[None, :])
    dmask = np.exp(dist[None] * log_gamma[:, None, None]) * visible[None]
    xi = np.exp((pos + 1.0)[None, :] * log_gamma[:, None])
    zeta = np.exp((tb - 1.0 - pos)[None, :] * log_gamma[:, None])
    gdec = np.exp(tb * log_gamma)
    lane = np.arange(2 * LANES) % LANES
    half = RET_DK // 2
    hm = np.stack([(lane // half == h) for h in range(RET_HEADS)]).astype(np.float64) * RET_DK ** -0.5
    npair = RET_HEADS // 2
    own = (hm > 0).astype(np.float64)
    xi_p = np.repeat(xi[:, :, None], RET_DV, axis=2).reshape(npair, 2, tb, RET_DV)
    xi_p = np.concatenate([xi_p[:, 0], xi_p[:, 1]], axis=2)
    zk = (zeta[:, :, None] * hm[:, None, :]).reshape(npair, 2, tb, 2 * LANES).sum(axis=1)
    col_head = np.repeat(np.arange(RET_HEADS).reshape(npair, 2), RET_DV, axis=1)
    smask = np.stack([own[col_head[p]].T for p in range(npair)])
    gdec_p = smask * gdec[col_head][:, None, :]
    tables = tuple(jnp.asarray(np.ascontiguousarray(a), F32) for a in (dmask, xi_p, zk, gdec_p, smask))
    return (jnp.asarray(hm[:, None, :], BF16),) + tables


def _front_weight(w_in, s5w):
    half = RET_DK // 2
    nqk = RET_HEADS * RET_DK
    w = w_in.astype(BF16)

    def regroup(cols):
        return cols.reshape(-1, RET_HEADS, 2, half).transpose(0, 2, 1, 3).reshape(-1, nqk)

    return jnp.concatenate([w[:, :s5w], regroup(w[:, s5w:s5w + nqk]),
                            regroup(w[:, s5w + nqk:s5w + 2 * nqk]), w[:, s5w + 2 * nqk:]], axis=1)


def _mixer_front(x, w_in, gn_gain, s5w):
    bsz, seq, d = x.shape
    tl = TL_FRONT
    ret_w = RET_HEADS * RET_DV
    half = RET_DK // 2
    freqs = np.float32(ROPE_BASE) ** (-np.arange(half, dtype=np.float32) / np.float32(half))
    ang = (np.arange(seq, dtype=np.float32)[:, None] * freqs[None, :]).astype(np.float64)
    cos = jnp.asarray(np.tile(np.cos(ang), (1, RET_HEADS)), F32)
    sin = jnp.asarray(np.tile(np.sin(ang), (1, RET_HEADS)), F32)
    hm, dmask, xi_p, zk, gdec_p, smask = _retention_consts(TB_RET)
    w = _front_weight(w_in, s5w)
    return pl.pallas_call(
        _front_kernel,
        grid=(bsz, seq // tl),
        in_specs=[
            pl.BlockSpec((1, tl, d), lambda b, i: (b, i, 0)),
            _const_spec(w.shape),
            pl.BlockSpec((tl, LANES), lambda b, i: (i, 0)),
            pl.BlockSpec((tl, LANES), lambda b, i: (i, 0)),
            _const_spec(hm.shape), _const_spec(dmask.shape), _const_spec(xi_p.shape),
            _const_spec(zk.shape), _const_spec(gdec_p.shape), _const_spec(smask.shape), _const_spec((1, ret_w)),
        ],
        out_specs=[
            pl.BlockSpec((1, tl, s5w), lambda b, i: (b, i, 0)),
            pl.BlockSpec((1, tl, ret_w), lambda b, i: (b, i, 0)),
        ],
        out_shape=[jax.ShapeDtypeStruct((bsz, seq, s5w), BF16),
                   jax.ShapeDtypeStruct((bsz, seq, ret_w), BF16)],
        scratch_shapes=[pltpu.VMEM((RET_HEADS // 2, 2 * LANES, 2 * RET_DV), F32)],
        compiler_params=pltpu.CompilerParams(
            dimension_semantics=("arbitrary", "arbitrary"), vmem_limit_bytes=VMEM_LIMIT_BYTES),
        name="mixer_front",
    )(x, w, cos, sin, hm, dmask, xi_p, zk, gdec_p, smask, gn_gain.reshape(1, ret_w).astype(F32))


def _same_group(shape, row_shift, col_shift):
    gpl_mask = LANES // S5_GROUP - 1
    rows = lax.broadcasted_iota(jnp.int32, shape, 0) >> row_shift
    cols = (lax.broadcasted_iota(jnp.int32, shape, 1) >> col_shift) & gpl_mask
    return (rows == cols).astype(F32)


def _s5_build(kt_ref, at_ref, ct_ref, t_ref, e_ref, c_ref):
    q = kt_ref.shape[1]
    gpl = LANES // S5_GROUP
    ch_shift = S5_GROUP.bit_length() - 1
    st_shift = S5_STATE.bit_length() - 1
    t_ref[...] = jnp.zeros_like(t_ref)
    mask_t = _same_group((LANES, LANES), ch_shift, ch_shift)
    for lag in range(q):
        bd = (jnp.concatenate([kt_ref[0, lag]] * gpl, axis=0) * mask_t).astype(BF16)
        for i in range(q - lag):
            j = i + lag
            t_ref[i * LANES:(i + 1) * LANES, j * LANES:(j + 1) * LANES] = bd
    mask_e = _same_group((LANES, e_ref.shape[1]), ch_shift, st_shift)
    for i in range(q):
        e_ref[i * LANES:(i + 1) * LANES, :] = (jnp.concatenate([at_ref[0, i]] * gpl, axis=0) * mask_e).astype(BF16)
    ns = c_ref.shape[0] // 2
    mask_c = _same_group((ns, c_ref.shape[1]), st_shift, ch_shift)
    for part in range(2):
        c_ref[part * ns:(part + 1) * ns, :] = (jnp.concatenate([ct_ref[0, part]] * gpl, axis=0) * mask_c).astype(BF16)


def _s5_kernel(u_ref, kt_ref, at_ref, ct_ref, aq_ref, y_ref, t_ref, e_ref, c_ref, sprev_ref, carry_ref):
    @pl.when(pl.program_id(1) == 0)
    def _():
        carry_ref[...] = jnp.zeros_like(carry_ref)
        _s5_build(kt_ref, at_ref, ct_ref, t_ref, e_ref, c_ref)

    nb, q, bsz, _ = u_ref.shape
    xcat = jnp.concatenate([u_ref[:, j].reshape(nb * bsz, LANES) for j in range(q)], axis=1)
    e = jnp.dot(xcat, e_ref[...], preferred_element_type=F32)
    ns = e.shape[1] // 2
    aqr = aq_ref[0, :, 0:ns]
    aqi = aq_ref[0, :, ns:2 * ns]
    s_re = carry_ref[:, 0:ns]
    s_im = carry_ref[:, ns:2 * ns]
    for n in range(nb):
        r0 = n * bsz
        sprev_ref[r0:r0 + bsz, :] = jnp.concatenate([s_re, s_im], axis=1).astype(BF16)
        e_re = e[r0:r0 + bsz, 0:ns]
        e_im = e[r0:r0 + bsz, ns:2 * ns]
        s_re, s_im = aqr * s_re - aqi * s_im + e_re, aqr * s_im + aqi * s_re + e_im
    carry_ref[...] = jnp.concatenate([s_re, s_im], axis=1)
    tw = 2 * LANES
    for jt in range(q * LANES // tw):
        hi = (jt + 1) * tw
        y = jnp.dot(xcat[:, 0:hi], t_ref[0:hi, jt * tw:hi], preferred_element_type=F32)
        y = y + jnp.dot(sprev_ref[...], c_ref[:, jt * tw:hi], preferred_element_type=F32)
        for j in range(jt * tw // LANES, hi // LANES):
            yj = y[:, j * LANES - jt * tw:(j + 1) * LANES - jt * tw]
            y_ref[:, j] = yj.reshape(nb, bsz, LANES).astype(BF16)


def _s5_matrices(lam_re, lam_im, b_re, b_im, c_re, c_im, d, log_step, q):
    g, p = lam_re.shape
    cch = b_re.shape[-1]
    gpl = LANES // cch
    nblk = g // gpl
    dt = jnp.exp(log_step)[:, None]
    zr, zi = lam_re * dt, lam_im * dt
    ks = jnp.arange(q + 1, dtype=F32)[:, None, None]
    mag = jnp.exp(ks * zr[None])
    pr, pi = mag * jnp.cos(ks * zi[None]), mag * jnp.sin(ks * zi[None])
    nr, ni = pr[1] - 1.0, pi[1]
    den = lam_re * lam_re + lam_im * lam_im
    fr = (nr * lam_re + ni * lam_im) / den
    fi = (ni * lam_re - nr * lam_im) / den
    hp = gpl * p
    lanes = lambda a: a.reshape(a.shape[:-2] + (nblk, hp))
    prb, pib, frb, fib = lanes(pr), lanes(pi), lanes(fr), lanes(fi)
    b_t = lambda b: b.reshape(nblk, gpl, p, cch).transpose(0, 3, 1, 2).reshape(nblk, cch, hp)
    c_t = lambda c: c.reshape(nblk, gpl, cch, p).transpose(0, 3, 1, 2)
    btr, bti = b_t(b_re), b_t(b_im)
    bbr = frb[:, None] * btr - fib[:, None] * bti
    bbi = frb[:, None] * bti + fib[:, None] * btr
    pre, pie = prb[q - 1::-1], pib[q - 1::-1]
    er = pre[:, :, None] * bbr[None] - pie[:, :, None] * bbi[None]
    ei = pre[:, :, None] * bbi[None] + pie[:, :, None] * bbr[None]
    at = jnp.concatenate([er, ei], axis=3).transpose(1, 0, 2, 3)
    cre, cim = c_re.reshape(nblk, gpl, cch, p), c_im.reshape(nblk, gpl, cch, p)
    er5, ei5 = er.reshape(q, nblk, cch, gpl, p), ei.reshape(q, nblk, cch, gpl, p)
    kern = (jnp.einsum('mhop,imchp->micho', cre, er5, precision=HI)
            - jnp.einsum('mhop,imchp->micho', cim, ei5, precision=HI))
    kt = kern[:, ::-1].reshape(nblk, q, cch, LANES)
    skip = d.reshape(nblk, gpl, cch).transpose(0, 2, 1)[..., None] * jnp.eye(cch, dtype=F32)[None, :, None, :]
    kt = kt.at[:, 0].add(skip.reshape(nblk, cch, LANES))
    crt, cit = c_t(c_re), c_t(c_im)
    prt = pr[1:].reshape(q, nblk, gpl, p).transpose(1, 3, 0, 2)[..., None]
    pit = pi[1:].reshape(q, nblk, gpl, p).transpose(1, 3, 0, 2)[..., None]
    cr = (crt[:, :, None] * prt - cit[:, :, None] * pit).reshape(nblk, p, q * LANES)
    ci = (crt[:, :, None] * pit + cit[:, :, None] * prt).reshape(nblk, p, q * LANES)
    ct = jnp.stack([cr, -ci], axis=1)
    aq = jnp.concatenate([prb[q], pib[q]], axis=1)
    return kt, at, ct, aq


def _s5_scan(u, mats):
    kt, at, ct, aq = mats
    bsz, seq, w = u.shape
    q, nb = S5_Q, S5_NB
    nblk = w // LANES
    ns2 = aq.shape[-1]
    u4 = jnp.transpose(u, (1, 0, 2)).reshape(seq // q, q, bsz, w)
    aq_b = jnp.broadcast_to(aq[:, None, :], (nblk, bsz, ns2))
    blk = pl.BlockSpec((nb, q, bsz, LANES), lambda m, i: (i, 0, 0, m))
    tab = lambda a: pl.BlockSpec((1,) + a.shape[1:], lambda m, i: (m,) + (0,) * (a.ndim - 1))
    y4 = pl.pallas_call(
        _s5_kernel,
        grid=(nblk, seq // (q * nb)),
        in_specs=[blk, tab(kt), tab(at), tab(ct), tab(aq_b)],
        out_specs=blk,
        out_shape=jax.ShapeDtypeStruct(u4.shape, BF16),
        scratch_shapes=[pltpu.VMEM((q * LANES, q * LANES), BF16),
                        pltpu.VMEM((q * LANES, ns2), BF16),
                        pltpu.VMEM((ns2, q * LANES), BF16),
                        pltpu.VMEM((nb * bsz, ns2), BF16),
                        pltpu.VMEM((bsz, ns2), F32)],
        compiler_params=pltpu.CompilerParams(
            dimension_semantics=("arbitrary", "arbitrary"), vmem_limit_bytes=VMEM_LIMIT_BYTES),
        name="s5_scan",
    )(u4, kt, at, ct, aq_b)
    return jnp.transpose(y4.reshape(seq, bsz, w), (1, 0, 2))


def _ffn_kernel(alpha, x_ref, ys_ref, yr_ref, gluw_ref, glub_ref, s5g_ref, wout_ref, ln1g_ref, ln1b_ref,
                wup_ref, cw_ref, cb_ref, wdn_ref, ln2g_ref, ln2b_ref, o_ref, hmid_ref, tail_ref):
    @pl.when(pl.program_id(1) == 0)
    def _():
        tail_ref[...] = jnp.zeros_like(tail_ref)

    s5w = ys_ref.shape[-1]
    dff = hmid_ref.shape[1]
    halo = tail_ref.shape[0]

    def mix(r0, ts):
        y = ys_ref[0, r0:r0 + ts, :].astype(F32)
        cdf = 0.5 * (1.0 + jnp.tanh(math.sqrt(2.0 / math.pi) * (y + 0.044715 * (y * y * y))))
        y = y * cdf
        z = jnp.dot(y.astype(BF16), gluw_ref[...], preferred_element_type=F32) + glub_ref[...]
        y = y * _sigmoid(z)
        y = y * lax.rsqrt(jnp.mean(y * y, axis=-1, keepdims=True) + LN_EPS) * s5g_ref[...]
        mix = jnp.dot(y.astype(BF16), wout_ref[0:s5w, :], preferred_element_type=F32)
        mix = mix + jnp.dot(yr_ref[0, r0:r0 + ts, :], wout_ref[s5w:, :], preferred_element_type=F32)
        return _layer_norm(alpha * x_ref[0, r0:r0 + ts, :] + mix, ln1g_ref[...], ln1b_ref[...])

    def chunk(r0, xb, c):
        ts = xb.shape[0]
        c0 = c * FC_FFN
        c1 = min(c0 + FC_FFN, dff)
        a = jnp.dot(xb, wup_ref[:, c0:c1], preferred_element_type=F32)
        g = jnp.dot(xb, wup_ref[:, dff + c0:dff + c1], preferred_element_type=F32)
        ext = jnp.concatenate([tail_ref[:, c0:c1], a], axis=0)
        tail_ref[:, c0:c1] = a[ts - halo:ts]
        a1 = ext[halo - 1:halo - 1 + ts]
        a2 = ext[halo - 2:halo - 2 + ts]
        cv = cw_ref[2:3, c0:c1] * a + cw_ref[1:2, c0:c1] * a1 + cw_ref[0:1, c0:c1] * a2 + cb_ref[:, c0:c1]
        hmid_ref[r0:r0 + ts, c0:c1] = (cv * _sigmoid(cv) * g).astype(BF16)

    def out(r0, res):
        o_ref[0, r0:r0 + res.shape[0], :] = _layer_norm(res, ln2g_ref[...], ln2b_ref[...])

    starts = [sum(FFN_SUBS[:i]) for i in range(len(FFN_SUBS))]
    x1 = mix(0, FFN_SUBS[0])
    res_prev = None
    for sub, (r0, ts) in enumerate(zip(starts, FFN_SUBS)):
        xb = x1.astype(BF16)
        x1_next = None
        for c in range(pl.cdiv(dff, FC_FFN)):
            chunk(r0, xb, c)
            if c == FFN_MIX_AT and sub + 1 < len(FFN_SUBS):
                x1_next = mix(r0 + ts, FFN_SUBS[sub + 1])
            if c == FFN_OUT_AT and res_prev is not None:
                out(starts[sub - 1], res_prev)
        res_prev = alpha * x1 + jnp.dot(hmid_ref[r0:r0 + ts, :], wdn_ref[...], preferred_element_type=F32)
        x1 = x1_next
    out(starts[-1], res_prev)


def _mixer_ffn(alpha, x, ys, yr, glu_w, glu_b, s5_gain, w_out, ln1_g, ln1_b,
               w_up, conv_w, conv_b, w_down, ln2_g, ln2_b):
    bsz, seq, d = x.shape
    tm = TM_FFN
    assert sum(FFN_SUBS) == tm and max(FFN_MIX_AT, FFN_OUT_AT) < pl.cdiv(w_down.shape[0], FC_FFN)
    s5w, ret_w = ys.shape[-1], yr.shape[-1]
    dff = w_down.shape[0]
    row = lambda a: a.reshape(1, -1).astype(F32)
    tok = lambda w: pl.BlockSpec((1, tm, w), lambda b, i: (b, i, 0))
    operands = [x, ys, yr, glu_w.astype(BF16), row(glu_b), row(s5_gain), w_out.astype(BF16),
                row(ln1_g), row(ln1_b), w_up.astype(BF16), conv_w.astype(F32), row(conv_b),
                w_down.astype(BF16), row(ln2_g), row(ln2_b)]
    in_specs = [tok(d), tok(s5w), tok(ret_w)] + [_const_spec(a.shape) for a in operands[3:]]
    return pl.pallas_call(
        functools.partial(_ffn_kernel, alpha),
        grid=(bsz, seq // tm),
        in_specs=in_specs,
        out_specs=tok(d),
        out_shape=jax.ShapeDtypeStruct((bsz, seq, d), x.dtype),
        scratch_shapes=[pltpu.VMEM((tm, dff), BF16), pltpu.VMEM((8, dff), F32)],
        compiler_params=pltpu.CompilerParams(
            dimension_semantics=("arbitrary", "arbitrary"), vmem_limit_bytes=VMEM_LIMIT_BYTES),
        name="mixer_ffn",
    )(*operands)


def kernel(x, w_in, s5_lambda_re, s5_lambda_im, s5_b_re, s5_b_im, s5_c_re, s5_c_im, s5_d, s5_log_step,
           s5_glu_w, s5_glu_b, s5_out_gain, ret_gn_gain, w_out, ln1_g, ln1_b, ffn_w_up, ffn_conv_w,
           ffn_conv_b, ffn_w_down, ln2_g, ln2_b):
    depth = w_in.shape[0]
    alpha = (2.0 * depth) ** 0.25
    s5w = s5_d.shape[-1]
    for l in range(depth):
        u, yr = _mixer_front(x, w_in[l], ret_gn_gain[l], s5w)
        mats = _s5_matrices(s5_lambda_re[l], s5_lambda_im[l], s5_b_re[l], s5_b_im[l], s5_c_re[l],
                            s5_c_im[l], s5_d[l], s5_log_step[l], S5_Q)
        ys = _s5_scan(u, mats)
        x = _mixer_ffn(alpha, x, ys, yr, s5_glu_w[l], s5_glu_b[l], s5_out_gain[l], w_out[l], ln1_g[l],
                       ln1_b[l], ffn_w_up[l], ffn_conv_w[l], ffn_conv_b[l], ffn_w_down[l], ln2_g[l], ln2_b[l])
    return x
```

```python
import functools
import math

import numpy as np
import jax
import jax.numpy as jnp
from jax import lax
from jax.experimental import pallas as pl
from jax.experimental.pallas import tpu as pltpu

CHUNK = 64
S5_GROUP = 16
S5_STATE = 64
RET_HEADS = 4
RET_DK = 64
RET_DV = 128
CONV_W = 3
ROPE_BASE = 10000.0
LN_EPS = 1e-5

LANES = 128
SUBLANES_BF16 = 16
VMEM_LIMIT_BYTES = 56 * 1024 * 1024

TL_FRONT = 2048
TB_RET = 256
S5_Q = 8
S5_NB = 64
TM_FFN = 512
FC_FFN = 256
FFN_SUBS = (256, 256)
FFN_MIX_AT = 1
FFN_OUT_AT = 6

F32 = jnp.float32
BF16 = jnp.bfloat16
HI = lax.Precision.HIGHEST


def _const_spec(shape):
    nd = len(shape)
    return pl.BlockSpec(shape, lambda *_: (0,) * nd, pipeline_mode=pl.Buffered(1))


def _layer_norm(v, g, b):
    mu = jnp.mean(v, axis=-1, keepdims=True)
    vc = v - mu
    var = jnp.mean(vc * vc, axis=-1, keepdims=True)
    return vc * lax.rsqrt(var + LN_EPS) * g + b


def _sigmoid(v):
    return 1.0 / (1.0 + jnp.exp(-v))


def _front_kernel(n_cast, x_ref, w_ref, cos_ref, sin_ref, hm_ref, dmask_ref, xi_ref, zk_ref,
                  gdec_ref, smask_ref, gn_ref, *rest):
    cast_in = rest[:n_cast]
    u_ref, yr_ref = rest[n_cast:n_cast + 2]
    cast_out = rest[n_cast + 2:2 * n_cast + 2]
    state_ref = rest[2 * n_cast + 2]
    @pl.when(pl.program_id(1) == 0)
    def _():
        state_ref[...] = jnp.zeros_like(state_ref)

    s5w = u_ref.shape[-1]
    xb = x_ref[0].astype(BF16)
    o_qk = s5w
    o_v = o_qk + 4 * LANES
    ret_w = RET_HEADS * RET_DV
    pw = 2 * RET_DV

    def project(r0):
        xr = xb[r0:r0 + TB_RET]
        qk = jnp.dot(xr, w_ref[:, o_qk:o_qk + 4 * LANES], preferred_element_type=F32)
        cos = cos_ref[r0:r0 + TB_RET, :]
        sin = sin_ref[r0:r0 + TB_RET, :]
        q1, q2 = qk[:, 0:LANES], qk[:, LANES:2 * LANES]
        k1, k2 = qk[:, 2 * LANES:3 * LANES], qk[:, 3 * LANES:4 * LANES]
        qb = jnp.concatenate([q1 * cos - q2 * sin, q1 * sin + q2 * cos], axis=1).astype(BF16)
        kb = jnp.concatenate([k1 * cos - k2 * sin, k1 * sin + k2 * cos], axis=1)
        v = jnp.dot(xr, w_ref[:, o_v:o_v + ret_w], preferred_element_type=F32).astype(BF16)
        gate = jnp.dot(xr, w_ref[:, o_v + ret_w:o_v + 2 * ret_w], preferred_element_type=F32)
        return qb, kb, v, gate

    def retain(r0, qb, kb, v, gate):
        kbb = kb.astype(BF16)
        for pair in range(RET_HEADS // 2):
            p0 = pair * pw
            st = state_ref[pair]
            cross = xi_ref[pair] * jnp.dot(qb, st.astype(BF16), preferred_element_type=F32)
            kz = (kb * zk_ref[pair]).astype(BF16)
            upd = lax.dot_general(kz, v[:, p0:p0 + pw], (((0,), (0,)), ((), ())), preferred_element_type=F32)
            state_ref[pair] = st * gdec_ref[pair] + upd * smask_ref[pair]
            for sub in range(2):
                hd = 2 * pair + sub
                c0 = hd * RET_DV
                kh = kbb * hm_ref[hd]
                sc = lax.dot_general(qb, kh, (((1,), (1,)), ((), ())), preferred_element_type=F32)
                p = (sc * dmask_ref[hd]).astype(BF16)
                o = jnp.dot(p, v[:, c0:c0 + RET_DV], preferred_element_type=F32)
                o = o + cross[:, sub * RET_DV:(sub + 1) * RET_DV]
                mu = jnp.mean(o, axis=-1, keepdims=True)
                oc = o - mu
                var = jnp.mean(oc * oc, axis=-1, keepdims=True)
                on = oc * lax.rsqrt(var + LN_EPS) * gn_ref[:, c0:c0 + RET_DV]
                g = gate[:, c0:c0 + RET_DV]
                yr_ref[0, r0:r0 + TB_RET, c0:c0 + RET_DV] = (on * (g * _sigmoid(g))).astype(BF16)

    n_blk = xb.shape[0] // TB_RET
    cur = project(0)
    for blk in range(n_blk):
        nxt = project((blk + 1) * TB_RET) if blk + 1 < n_blk else None
        retain(blk * TB_RET, *cur)
        cur = nxt

    u_ref[0] = jnp.dot(xb, w_ref[:, 0:s5w], preferred_element_type=F32).astype(BF16)

    for src, dst in zip(cast_in, cast_out):
        dst[...] = src[...].astype(BF16)


def _retention_consts(tb):
    log_gamma = np.log1p(-(2.0 ** (-5.0 - np.arange(RET_HEADS, dtype=np.float64))))
    pos = np.arange(tb, dtype=np.float64)
    chunk = np.arange(tb) // CHUNK
    visible = (chunk[None, :] <= chunk[:, None]).astype(np.float64)
    dist = np.abs(pos[:, None] - pos[None, :])
    dmask = np.exp(dist[None] * log_gamma[:, None, None]) * visible[None]
    xi = np.exp((pos + 1.0)[None, :] * log_gamma[:, None])
    zeta = np.exp((tb - 1.0 - pos)[None, :] * log_gamma[:, None])
    gdec = np.exp(tb * log_gamma)
    lane = np.arange(2 * LANES) % LANES
    half = RET_DK // 2
    hm = np.stack([(lane // half == h) for h in range(RET_HEADS)]).astype(np.float64) * RET_DK ** -0.5
    npair = RET_HEADS // 2
    own = (hm > 0).astype(np.float64)
    xi_p = np.repeat(xi[:, :, None], RET_DV, axis=2).reshape(npair, 2, tb, RET_DV)
    xi_p = np.concatenate([xi_p[:, 0], xi_p[:, 1]], axis=2)
    zk = (zeta[:, :, None] * hm[:, None, :]).reshape(npair, 2, tb, 2 * LANES).sum(axis=1)
    col_head = np.repeat(np.arange(RET_HEADS).reshape(npair, 2), RET_DV, axis=1)
    smask = np.stack([own[col_head[p]].T for p in range(npair)])
    gdec_p = smask * gdec[col_head][:, None, :]
    tables = tuple(jnp.asarray(np.ascontiguousarray(a), F32) for a in (dmask, xi_p, zk, gdec_p, smask))
    return (jnp.asarray(hm[:, None, :], BF16),) + tables


def _front_weight(w_in, s5w):
    half = RET_DK // 2
    nqk = RET_HEADS * RET_DK
    w = w_in.astype(BF16)

    def regroup(cols):
        return cols.reshape(-1, RET_HEADS, 2, half).transpose(0, 2, 1, 3).reshape(-1, nqk)

    return jnp.concatenate([w[:, :s5w], regroup(w[:, s5w:s5w + nqk]),
                            regroup(w[:, s5w + nqk:s5w + 2 * nqk]), w[:, s5w + 2 * nqk:]], axis=1)


def _mixer_front(x, w_in, gn_gain, s5w, later_weights):
    bsz, seq, d = x.shape
    tl = TL_FRONT
    ret_w = RET_HEADS * RET_DV
    half = RET_DK // 2
    freqs = np.float32(ROPE_BASE) ** (-np.arange(half, dtype=np.float32) / np.float32(half))
    ang = (np.arange(seq, dtype=np.float32)[:, None] * freqs[None, :]).astype(np.float64)
    cos = jnp.asarray(np.tile(np.cos(ang), (1, RET_HEADS)), F32)
    sin = jnp.asarray(np.tile(np.sin(ang), (1, RET_HEADS)), F32)
    hm, dmask, xi_p, zk, gdec_p, smask = _retention_consts(TB_RET)
    w = _front_weight(w_in, s5w)
    rows = lambda a: pl.BlockSpec((a.shape[0] // bsz, a.shape[1]), lambda b, i: (b, 0))
    assert all(a.shape[0] % (bsz * SUBLANES_BF16) == 0 for a in later_weights)
    return pl.pallas_call(
        functools.partial(_front_kernel, len(later_weights)),
        grid=(bsz, seq // tl),
        in_specs=[
            pl.BlockSpec((1, tl, d), lambda b, i: (b, i, 0)),
            _const_spec(w.shape),
            pl.BlockSpec((tl, LANES), lambda b, i: (i, 0)),
            pl.BlockSpec((tl, LANES), lambda b, i: (i, 0)),
            _const_spec(hm.shape), _const_spec(dmask.shape), _const_spec(xi_p.shape),
            _const_spec(zk.shape), _const_spec(gdec_p.shape), _const_spec(smask.shape), _const_spec((1, ret_w)),
        ] + [rows(a) for a in later_weights],
        out_specs=[
            pl.BlockSpec((1, tl, s5w), lambda b, i: (b, i, 0)),
            pl.BlockSpec((1, tl, ret_w), lambda b, i: (b, i, 0)),
        ] + [rows(a) for a in later_weights],
        out_shape=[jax.ShapeDtypeStruct((bsz, seq, s5w), BF16),
                   jax.ShapeDtypeStruct((bsz, seq, ret_w), BF16)]
                  + [jax.ShapeDtypeStruct(a.shape, BF16) for a in later_weights],
        scratch_shapes=[pltpu.VMEM((RET_HEADS // 2, 2 * LANES, 2 * RET_DV), F32)],
        compiler_params=pltpu.CompilerParams(
            dimension_semantics=("arbitrary", "arbitrary"), vmem_limit_bytes=VMEM_LIMIT_BYTES),
        name="mixer_front",
    )(x, w, cos, sin, hm, dmask, xi_p, zk, gdec_p, smask, gn_gain.reshape(1, ret_w).astype(F32), *later_weights)


def _same_group(shape, row_shift, col_shift):
    gpl_mask = LANES // S5_GROUP - 1
    rows = lax.broadcasted_iota(jnp.int32, shape, 0) >> row_shift
    cols = (lax.broadcasted_iota(jnp.int32, shape, 1) >> col_shift) & gpl_mask
    return (rows == cols).astype(F32)


def _s5_build(kt_ref, at_ref, ct_ref, t_ref, e_ref, c_ref):
    q = kt_ref.shape[1]
    gpl = LANES // S5_GROUP
    ch_shift = S5_GROUP.bit_length() - 1
    st_shift = S5_STATE.bit_length() - 1
    t_ref[...] = jnp.zeros_like(t_ref)
    mask_t = _same_group((LANES, LANES), ch_shift, ch_shift)
    for lag in range(q):
        bd = (jnp.concatenate([kt_ref[0, lag]] * gpl, axis=0) * mask_t).astype(BF16)
        for i in range(q - lag):
            j = i + lag
            t_ref[i * LANES:(i + 1) * LANES, j * LANES:(j + 1) * LANES] = bd
    mask_e = _same_group((LANES, e_ref.shape[1]), ch_shift, st_shift)
    for i in range(q):
        e_ref[i * LANES:(i + 1) * LANES, :] = (jnp.concatenate([at_ref[0, i]] * gpl, axis=0) * mask_e).astype(BF16)
    ns = c_ref.shape[0] // 2
    mask_c = _same_group((ns, c_ref.shape[1]), st_shift, ch_shift)
    for part in range(2):
        c_ref[part * ns:(part + 1) * ns, :] = (jnp.concatenate([ct_ref[0, part]] * gpl, axis=0) * mask_c).astype(BF16)


def _s5_kernel(u_ref, kt_ref, at_ref, ct_ref, aq_ref, y_ref, t_ref, e_ref, c_ref, sprev_ref, carry_ref):
    @pl.when(pl.program_id(1) == 0)
    def _():
        carry_ref[...] = jnp.zeros_like(carry_ref)
        _s5_build(kt_ref, at_ref, ct_ref, t_ref, e_ref, c_ref)

    nb, q, bsz, _ = u_ref.shape
    xcat = jnp.concatenate([u_ref[:, j].reshape(nb * bsz, LANES) for j in range(q)], axis=1)
    e = jnp.dot(xcat, e_ref[...], preferred_element_type=F32)
    ns = e.shape[1] // 2
    aqr = aq_ref[0, :, 0:ns]
    aqi = aq_ref[0, :, ns:2 * ns]
    s_re = carry_ref[:, 0:ns]
    s_im = carry_ref[:, ns:2 * ns]
    for n in range(nb):
        r0 = n * bsz
        sprev_ref[r0:r0 + bsz, :] = jnp.concatenate([s_re, s_im], axis=1).astype(BF16)
        e_re = e[r0:r0 + bsz, 0:ns]
        e_im = e[r0:r0 + bsz, ns:2 * ns]
        s_re, s_im = aqr * s_re - aqi * s_im + e_re, aqr * s_im + aqi * s_re + e_im
    carry_ref[...] = jnp.concatenate([s_re, s_im], axis=1)
    tw = 2 * LANES
    for jt in range(q * LANES // tw):
        hi = (jt + 1) * tw
        y = jnp.dot(xcat[:, 0:hi], t_ref[0:hi, jt * tw:hi], preferred_element_type=F32)
        y = y + jnp.dot(sprev_ref[...], c_ref[:, jt * tw:hi], preferred_element_type=F32)
        for j in range(jt * tw // LANES, hi // LANES):
            yj = y[:, j * LANES - jt * tw:(j + 1) * LANES - jt * tw]
            y_ref[:, j] = yj.reshape(nb, bsz, LANES).astype(BF16)


def _s5_matrices(lam_re, lam_im, b_re, b_im, c_re, c_im, d, log_step, q):
    g, p = lam_re.shape
    cch = b_re.shape[-1]
    gpl = LANES // cch
    nblk = g // gpl
    dt = jnp.exp(log_step)[:, None]
    zr, zi = lam_re * dt, lam_im * dt
    ks = jnp.arange(q + 1, dtype=F32)[:, None, None]
    mag = jnp.exp(ks * zr[None])
    pr, pi = mag * jnp.cos(ks * zi[None]), mag * jnp.sin(ks * zi[None])
    nr, ni = pr[1] - 1.0, pi[1]
    den = lam_re * lam_re + lam_im * lam_im
    fr = (nr * lam_re + ni * lam_im) / den
    fi = (ni * lam_re - nr * lam_im) / den
    hp = gpl * p
    lanes = lambda a: a.reshape(a.shape[:-2] + (nblk, hp))
    prb, pib, frb, fib = lanes(pr), lanes(pi), lanes(fr), lanes(fi)
    b_t = lambda b: b.reshape(nblk, gpl, p, cch).transpose(0, 3, 1, 2).reshape(nblk, cch, hp)
    c_t = lambda c: c.reshape(nblk, gpl, cch, p).transpose(0, 3, 1, 2)
    btr, bti = b_t(b_re), b_t(b_im)
    bbr = frb[:, None] * btr - fib[:, None] * bti
    bbi = frb[:, None] * bti + fib[:, None] * btr
    pre, pie = prb[q - 1::-1], pib[q - 1::-1]
    er = pre[:, :, None] * bbr[None] - pie[:, :, None] * bbi[None]
    ei = pre[:, :, None] * bbi[None] + pie[:, :, None] * bbr[None]
    at = jnp.concatenate([er, ei], axis=3).transpose(1, 0, 2, 3)
    cre, cim = c_re.reshape(nblk, gpl, cch, p), c_im.reshape(nblk, gpl, cch, p)
    er5, ei5 = er.reshape(q, nblk, cch, gpl, p), ei.reshape(q, nblk, cch, gpl, p)
    kern = (jnp.einsum('mhop,imchp->micho', cre, er5, precision=HI)
            - jnp.einsum('mhop,imchp->micho', cim, ei5, precision=HI))
    kt = kern[:, ::-1].reshape(nblk, q, cch, LANES)
    skip = d.reshape(nblk, gpl, cch).transpose(0, 2, 1)[..., None] * jnp.eye(cch, dtype=F32)[None, :, None, :]
    kt = kt.at[:, 0].add(skip.reshape(nblk, cch, LANES))
    crt, cit = c_t(c_re), c_t(c_im)
    prt = pr[1:].reshape(q, nblk, gpl, p).transpose(1, 3, 0, 2)[..., None]
    pit = pi[1:].reshape(q, nblk, gpl, p).transpose(1, 3, 0, 2)[..., None]
    cr = (crt[:, :, None] * prt - cit[:, :, None] * pit).reshape(nblk, p, q * LANES)
    ci = (crt[:, :, None] * pit + cit[:, :, None] * prt).reshape(nblk, p, q * LANES)
    ct = jnp.stack([cr, -ci], axis=1)
    aq = jnp.concatenate([prb[q], pib[q]], axis=1)
    return kt, at, ct, aq


def _s5_scan(u, mats):
    kt, at, ct, aq = mats
    bsz, seq, w = u.shape
    q, nb = S5_Q, S5_NB
    nblk = w // LANES
    ns2 = aq.shape[-1]
    u4 = jnp.transpose(u, (1, 0, 2)).reshape(seq // q, q, bsz, w)
    aq_b = jnp.broadcast_to(aq[:, None, :], (nblk, bsz, ns2))
    blk = pl.BlockSpec((nb, q, bsz, LANES), lambda m, i: (i, 0, 0, m))
    tab = lambda a: pl.BlockSpec((1,) + a.shape[1:], lambda m, i: (m,) + (0,) * (a.ndim - 1))
    y4 = pl.pallas_call(
        _s5_kernel,
        grid=(nblk, seq // (q * nb)),
        in_specs=[blk, tab(kt), tab(at), tab(ct), tab(aq_b)],
        out_specs=blk,
        out_shape=jax.ShapeDtypeStruct(u4.shape, BF16),
        scratch_shapes=[pltpu.VMEM((q * LANES, q * LANES), BF16),
                        pltpu.VMEM((q * LANES, ns2), BF16),
                        pltpu.VMEM((ns2, q * LANES), BF16),
                        pltpu.VMEM((nb * bsz, ns2), BF16),
                        pltpu.VMEM((bsz, ns2), F32)],
        compiler_params=pltpu.CompilerParams(
            dimension_semantics=("arbitrary", "arbitrary"), vmem_limit_bytes=VMEM_LIMIT_BYTES),
        name="s5_scan",
    )(u4, kt, at, ct, aq_b)
    return jnp.transpose(y4.reshape(seq, bsz, w), (1, 0, 2))


def _ffn_kernel(alpha, x_ref, ys_ref, yr_ref, gluw_ref, glub_ref, s5g_ref, wout_ref, ln1g_ref, ln1b_ref,
                wup_ref, cw_ref, cb_ref, wdn_ref, ln2g_ref, ln2b_ref, o_ref, hmid_ref, tail_ref):
    @pl.when(pl.program_id(1) == 0)
    def _():
        tail_ref[...] = jnp.zeros_like(tail_ref)

    s5w = ys_ref.shape[-1]
    dff = hmid_ref.shape[1]
    halo = tail_ref.shape[0]

    def mix(r0, ts):
        y = ys_ref[0, r0:r0 + ts, :].astype(F32)
        cdf = 0.5 * (1.0 + jnp.tanh(math.sqrt(2.0 / math.pi) * (y + 0.044715 * (y * y * y))))
        y = y * cdf
        z = jnp.dot(y.astype(BF16), gluw_ref[...], preferred_element_type=F32) + glub_ref[...]
        y = y * _sigmoid(z)
        y = y * lax.rsqrt(jnp.mean(y * y, axis=-1, keepdims=True) + LN_EPS) * s5g_ref[...]
        mix = jnp.dot(y.astype(BF16), wout_ref[0:s5w, :], preferred_element_type=F32)
        mix = mix + jnp.dot(yr_ref[0, r0:r0 + ts, :], wout_ref[s5w:, :], preferred_element_type=F32)
        return _layer_norm(alpha * x_ref[0, r0:r0 + ts, :] + mix, ln1g_ref[...], ln1b_ref[...])

    def chunk(r0, xb, c):
        ts = xb.shape[0]
        c0 = c * FC_FFN
        c1 = min(c0 + FC_FFN, dff)
        a = jnp.dot(xb, wup_ref[:, c0:c1], preferred_element_type=F32)
        g = jnp.dot(xb, wup_ref[:, dff + c0:dff + c1], preferred_element_type=F32)
        ext = jnp.concatenate([tail_ref[:, c0:c1], a], axis=0)
        tail_ref[:, c0:c1] = a[ts - halo:ts]
        a1 = ext[halo - 1:halo - 1 + ts]
        a2 = ext[halo - 2:halo - 2 + ts]
        cv = cw_ref[2:3, c0:c1] * a + cw_ref[1:2, c0:c1] * a1 + cw_ref[0:1, c0:c1] * a2 + cb_ref[:, c0:c1]
        hmid_ref[r0:r0 + ts, c0:c1] = (cv * _sigmoid(cv) * g).astype(BF16)

    def out(r0, res):
        o_ref[0, r0:r0 + res.shape[0], :] = _layer_norm(res, ln2g_ref[...], ln2b_ref[...])

    starts = [sum(FFN_SUBS[:i]) for i in range(len(FFN_SUBS))]
    x1 = mix(0, FFN_SUBS[0])
    res_prev = None
    for sub, (r0, ts) in enumerate(zip(starts, FFN_SUBS)):
        xb = x1.astype(BF16)
        x1_next = None
        for c in range(pl.cdiv(dff, FC_FFN)):
            chunk(r0, xb, c)
            if c == FFN_MIX_AT and sub + 1 < len(FFN_SUBS):
                x1_next = mix(r0 + ts, FFN_SUBS[sub + 1])
            if c == FFN_OUT_AT and res_prev is not None:
                out(starts[sub - 1], res_prev)
        res_prev = alpha * x1 + jnp.dot(hmid_ref[r0:r0 + ts, :], wdn_ref[...], preferred_element_type=F32)
        x1 = x1_next
    out(starts[-1], res_prev)


def _mixer_ffn(alpha, x, ys, yr, glu_w, glu_b, s5_gain, w_out, ln1_g, ln1_b,
               w_up, conv_w, conv_b, w_down, ln2_g, ln2_b):
    bsz, seq, d = x.shape
    tm = TM_FFN
    assert sum(FFN_SUBS) == tm and max(FFN_MIX_AT, FFN_OUT_AT) < pl.cdiv(w_down.shape[0], FC_FFN)
    s5w, ret_w = ys.shape[-1], yr.shape[-1]
    dff = w_down.shape[0]
    row = lambda a: a.reshape(1, -1).astype(F32)
    tok = lambda w: pl.BlockSpec((1, tm, w), lambda b, i: (b, i, 0))
    assert all(w.dtype == BF16 for w in (glu_w, w_out, w_up, w_down))
    operands = [x, ys, yr, glu_w, row(glu_b), row(s5_gain), w_out, row(ln1_g), row(ln1_b), w_up,
                conv_w.astype(F32), row(conv_b), w_down, row(ln2_g), row(ln2_b)]
    in_specs = [tok(d), tok(s5w), tok(ret_w)] + [_const_spec(a.shape) for a in operands[3:]]
    return pl.pallas_call(
        functools.partial(_ffn_kernel, alpha),
        grid=(bsz, seq // tm),
        in_specs=in_specs,
        out_specs=tok(d),
        out_shape=jax.ShapeDtypeStruct((bsz, seq, d), x.dtype),
        scratch_shapes=[pltpu.VMEM((tm, dff), BF16), pltpu.VMEM((8, dff), F32)],
        compiler_params=pltpu.CompilerParams(
            dimension_semantics=("arbitrary", "arbitrary"), vmem_limit_bytes=VMEM_LIMIT_BYTES),
        name="mixer_ffn",
    )(*operands)


def kernel(x, w_in, s5_lambda_re, s5_lambda_im, s5_b_re, s5_b_im, s5_c_re, s5_c_im, s5_d, s5_log_step,
           s5_glu_w, s5_glu_b, s5_out_gain, ret_gn_gain, w_out, ln1_g, ln1_b, ffn_w_up, ffn_conv_w,
           ffn_conv_b, ffn_w_down, ln2_g, ln2_b):
    depth = w_in.shape[0]
    alpha = (2.0 * depth) ** 0.25
    s5w = s5_d.shape[-1]
    for l in range(depth):
        u, yr, glu_w, w_o, w_up, w_down = _mixer_front(
            x, w_in[l], ret_gn_gain[l], s5w, [s5_glu_w[l], w_out[l], ffn_w_up[l], ffn_w_down[l]])
        mats = _s5_matrices(s5_lambda_re[l], s5_lambda_im[l], s5_b_re[l], s5_b_im[l], s5_c_re[l],
                            s5_c_im[l], s5_d[l], s5_log_step[l], S5_Q)
        ys = _s5_scan(u, mats)
        x = _mixer_ffn(alpha, x, ys, yr, glu_w, s5_glu_b[l], s5_out_gain[l], w_o, ln1_g[l],
                       ln1_b[l], w_up, ffn_conv_w[l], ffn_conv_b[l], w_down, ln2_g[l], ln2_b[l])
    return x
```

```python
import functools
import math

import numpy as np
import jax
import jax.numpy as jnp
from jax import lax
from jax.experimental import pallas as pl
from jax.experimental.pallas import tpu as pltpu

CHUNK = 64
S5_GROUP = 16
S5_STATE = 64
RET_HEADS = 4
RET_DK = 64
RET_DV = 128
CONV_W = 3
ROPE_BASE = 10000.0
LN_EPS = 1e-5

LANES = 128
SUBLANES_BF16 = 16
VMEM_LIMIT_BYTES = 56 * 1024 * 1024

TL_FRONT = 2048
TB_RET = 256
S5_Q = 8
S5_NB = 64
TM_FFN = 512
FC_FFN = 256
FFN_SUBS = (256, 256)
FFN_MIX_AT = 1
FFN_OUT_AT = 6

F32 = jnp.float32
BF16 = jnp.bfloat16
HI = lax.Precision.HIGHEST


def _const_spec(shape):
    nd = len(shape)
    return pl.BlockSpec(shape, lambda *_: (0,) * nd, pipeline_mode=pl.Buffered(1))


def _layer_norm(v, g, b):
    mu = jnp.mean(v, axis=-1, keepdims=True)
    vc = v - mu
    var = jnp.mean(vc * vc, axis=-1, keepdims=True)
    return vc * lax.rsqrt(var + LN_EPS) * g + b


def _sigmoid(v):
    return 1.0 / (1.0 + jnp.exp(-v))


def _front_kernel(n_cast, x_ref, w_ref, cos_ref, sin_ref, hm_ref, dmask_ref, xi_ref, zk_ref,
                  gdec_ref, smask_ref, gn_ref, *rest):
    cast_in = rest[:n_cast]
    u_ref, yr_ref = rest[n_cast:n_cast + 2]
    cast_out = rest[n_cast + 2:2 * n_cast + 2]
    state_ref = rest[2 * n_cast + 2]
    @pl.when(pl.program_id(1) == 0)
    def _():
        state_ref[...] = jnp.zeros_like(state_ref)

    s5w = u_ref.shape[-1]
    xb = x_ref[0].astype(BF16)
    o_qk = s5w
    o_v = o_qk + 4 * LANES
    ret_w = RET_HEADS * RET_DV
    pw = 2 * RET_DV

    def project(r0):
        xr = xb[r0:r0 + TB_RET]
        qk = jnp.dot(xr, w_ref[:, o_qk:o_qk + 4 * LANES], preferred_element_type=F32)
        cos = cos_ref[r0:r0 + TB_RET, :]
        sin = sin_ref[r0:r0 + TB_RET, :]
        q1, q2 = qk[:, 0:LANES], qk[:, LANES:2 * LANES]
        k1, k2 = qk[:, 2 * LANES:3 * LANES], qk[:, 3 * LANES:4 * LANES]
        qb = jnp.concatenate([q1 * cos - q2 * sin, q1 * sin + q2 * cos], axis=1).astype(BF16)
        kb = jnp.concatenate([k1 * cos - k2 * sin, k1 * sin + k2 * cos], axis=1)
        v = jnp.dot(xr, w_ref[:, o_v:o_v + ret_w], preferred_element_type=F32).astype(BF16)
        gate = jnp.dot(xr, w_ref[:, o_v + ret_w:o_v + 2 * ret_w], preferred_element_type=F32)
        return qb, kb, v, gate

    def retain(r0, qb, kb, v, gate):
        kbb = kb.astype(BF16)
        for pair in range(RET_HEADS // 2):
            p0 = pair * pw
            st = state_ref[pair]
            cross = xi_ref[pair] * jnp.dot(qb, st.astype(BF16), preferred_element_type=F32)
            kz = (kb * zk_ref[pair]).astype(BF16)
            upd = lax.dot_general(kz, v[:, p0:p0 + pw], (((0,), (0,)), ((), ())), preferred_element_type=F32)
            state_ref[pair] = st * gdec_ref[pair] + upd * smask_ref[pair]
            for sub in range(2):
                hd = 2 * pair + sub
                c0 = hd * RET_DV
                kh = kbb * hm_ref[hd]
                sc = lax.dot_general(qb, kh, (((1,), (1,)), ((), ())), preferred_element_type=F32)
                p = (sc * dmask_ref[hd]).astype(BF16)
                o = jnp.dot(p, v[:, c0:c0 + RET_DV], preferred_element_type=F32)
                o = o + cross[:, sub * RET_DV:(sub + 1) * RET_DV]
                mu = jnp.mean(o, axis=-1, keepdims=True)
                oc = o - mu
                var = jnp.mean(oc * oc, axis=-1, keepdims=True)
                on = oc * lax.rsqrt(var + LN_EPS) * gn_ref[:, c0:c0 + RET_DV]
                g = gate[:, c0:c0 + RET_DV]
                yr_ref[0, r0:r0 + TB_RET, c0:c0 + RET_DV] = (on * (g * _sigmoid(g))).astype(BF16)

    n_blk = xb.shape[0] // TB_RET
    cur = project(0)
    for blk in range(n_blk):
        nxt = project((blk + 1) * TB_RET) if blk + 1 < n_blk else None
        retain(blk * TB_RET, *cur)
        cur = nxt

    u_ref[0] = jnp.dot(xb, w_ref[:, 0:s5w], preferred_element_type=F32).astype(BF16)

    for src, dst in zip(cast_in, cast_out):
        dst[...] = src[...].astype(BF16)


def _retention_consts(tb):
    log_gamma = np.log1p(-(2.0 ** (-5.0 - np.arange(RET_HEADS, dtype=np.float64))))
    pos = np.arange(tb, dtype=np.float64)
    chunk = np.arange(tb) // CHUNK
    visible = (chunk[None, :] <= chunk[:, None]).astype(np.float64)
    dist = np.abs(pos[:, None] - pos[None, :])
    dmask = np.exp(dist[None] * log_gamma[:, None, None]) * visible[None]
    xi = np.exp((pos + 1.0)[None, :] * log_gamma[:, None])
    zeta = np.exp((tb - 1.0 - pos)[None, :] * log_gamma[:, None])
    gdec = np.exp(tb * log_gamma)
    lane = np.arange(2 * LANES) % LANES
    half = RET_DK // 2
    hm = np.stack([(lane // half == h) for h in range(RET_HEADS)]).astype(np.float64) * RET_DK ** -0.5
    npair = RET_HEADS // 2
    own = (hm > 0).astype(np.float64)
    xi_p = np.repeat(xi[:, :, None], RET_DV, axis=2).reshape(npair, 2, tb, RET_DV)
    xi_p = np.concatenate([xi_p[:, 0], xi_p[:, 1]], axis=2)
    zk = (zeta[:, :, None] * hm[:, None, :]).reshape(npair, 2, tb, 2 * LANES).sum(axis=1)
    col_head = np.repeat(np.arange(RET_HEADS).reshape(npair, 2), RET_DV, axis=1)
    smask = np.stack([own[col_head[p]].T for p in range(npair)])
    gdec_p = smask * gdec[col_head][:, None, :]
    tables = tuple(jnp.asarray(np.ascontiguousarray(a), F32) for a in (dmask, xi_p, zk, gdec_p, smask))
    return (jnp.asarray(hm[:, None, :], BF16),) + tables


def _front_weight(w_in, s5w):
    half = RET_DK // 2
    nqk = RET_HEADS * RET_DK
    w = w_in.astype(BF16)

    def regroup(cols):
        return cols.reshape(-1, RET_HEADS, 2, half).transpose(0, 2, 1, 3).reshape(-1, nqk)

    return jnp.concatenate([w[:, :s5w], regroup(w[:, s5w:s5w + nqk]),
                            regroup(w[:, s5w + nqk:s5w + 2 * nqk]), w[:, s5w + 2 * nqk:]], axis=1)


def _mixer_front(x, w_in, gn_gain, s5w, later_weights):
    bsz, seq, d = x.shape
    tl = TL_FRONT
    ret_w = RET_HEADS * RET_DV
    half = RET_DK // 2
    freqs = np.float32(ROPE_BASE) ** (-np.arange(half, dtype=np.float32) / np.float32(half))
    ang = (np.arange(seq, dtype=np.float32)[:, None] * freqs[None, :]).astype(np.float64)
    cos = jnp.asarray(np.tile(np.cos(ang), (1, RET_HEADS)), F32)
    sin = jnp.asarray(np.tile(np.sin(ang), (1, RET_HEADS)), F32)
    hm, dmask, xi_p, zk, gdec_p, smask = _retention_consts(TB_RET)
    w = _front_weight(w_in, s5w)
    rows = lambda a: pl.BlockSpec((a.shape[0] // bsz, a.shape[1]), lambda b, i: (b, 0))
    assert all(a.shape[0] % (bsz * SUBLANES_BF16) == 0 for a in later_weights)
    return pl.pallas_call(
        functools.partial(_front_kernel, len(later_weights)),
        grid=(bsz, seq // tl),
        in_specs=[
            pl.BlockSpec((1, tl, d), lambda b, i: (b, i, 0)),
            _const_spec(w.shape),
            pl.BlockSpec((tl, LANES), lambda b, i: (i, 0)),
            pl.BlockSpec((tl, LANES), lambda b, i: (i, 0)),
            _const_spec(hm.shape), _const_spec(dmask.shape), _const_spec(xi_p.shape),
            _const_spec(zk.shape), _const_spec(gdec_p.shape), _const_spec(smask.shape), _const_spec((1, ret_w)),
        ] + [rows(a) for a in later_weights],
        out_specs=[
            pl.BlockSpec((1, tl, s5w), lambda b, i: (b, i, 0)),
            pl.BlockSpec((1, tl, ret_w), lambda b, i: (b, i, 0)),
        ] + [rows(a) for a in later_weights],
        out_shape=[jax.ShapeDtypeStruct((bsz, seq, s5w), BF16),
                   jax.ShapeDtypeStruct((bsz, seq, ret_w), BF16)]
                  + [jax.ShapeDtypeStruct(a.shape, BF16) for a in later_weights],
        scratch_shapes=[pltpu.VMEM((RET_HEADS // 2, 2 * LANES, 2 * RET_DV), F32)],
        compiler_params=pltpu.CompilerParams(
            dimension_semantics=("arbitrary", "arbitrary"), vmem_limit_bytes=VMEM_LIMIT_BYTES),
        name="mixer_front",
    )(x, w, cos, sin, hm, dmask, xi_p, zk, gdec_p, smask, gn_gain.reshape(1, ret_w).astype(F32), *later_weights)


def _same_group(shape, row_shift, col_shift):
    gpl_mask = LANES // S5_GROUP - 1
    rows = lax.broadcasted_iota(jnp.int32, shape, 0) >> row_shift
    cols = (lax.broadcasted_iota(jnp.int32, shape, 1) >> col_shift) & gpl_mask
    return (rows == cols).astype(F32)


def _s5_build(kt_ref, at_ref, ct_ref, t_ref, e_ref, c_ref):
    q = kt_ref.shape[1]
    gpl = LANES // S5_GROUP
    ch_shift = S5_GROUP.bit_length() - 1
    st_shift = S5_STATE.bit_length() - 1
    t_ref[...] = jnp.zeros_like(t_ref)
    mask_t = _same_group((LANES, LANES), ch_shift, ch_shift)
    for lag in range(q):
        bd = (jnp.concatenate([kt_ref[0, lag]] * gpl, axis=0) * mask_t).astype(BF16)
        for i in range(q - lag):
            j = i + lag
            t_ref[i * LANES:(i + 1) * LANES, j * LANES:(j + 1) * LANES] = bd
    mask_e = _same_group((LANES, e_ref.shape[1]), ch_shift, st_shift)
    for i in range(q):
        e_ref[i * LANES:(i + 1) * LANES, :] = (jnp.concatenate([at_ref[0, i]] * gpl, axis=0) * mask_e).astype(BF16)
    ns = c_ref.shape[0] // 2
    mask_c = _same_group((ns, c_ref.shape[1]), st_shift, ch_shift)
    for part in range(2):
        c_ref[part * ns:(part + 1) * ns, :] = (jnp.concatenate([ct_ref[0, part]] * gpl, axis=0) * mask_c).astype(BF16)


def _s5_kernel(u_ref, kt_ref, at_ref, ct_ref, aq_ref, y_ref, t_ref, e_ref, c_ref, sprev_ref, carry_ref):
    @pl.when(pl.program_id(1) == 0)
    def _():
        carry_ref[...] = jnp.zeros_like(carry_ref)
        _s5_build(kt_ref, at_ref, ct_ref, t_ref, e_ref, c_ref)

    nb, q, bsz, _ = u_ref.shape
    xcat = jnp.concatenate([u_ref[:, j].reshape(nb * bsz, LANES) for j in range(q)], axis=1)
    e = jnp.dot(xcat, e_ref[...], preferred_element_type=F32)
    ns = e.shape[1] // 2
    aqr = aq_ref[0, :, 0:ns]
    aqi = aq_ref[0, :, ns:2 * ns]
    s_re = carry_ref[:, 0:ns]
    s_im = carry_ref[:, ns:2 * ns]
    for n in range(nb):
        r0 = n * bsz
        sprev_ref[r0:r0 + bsz, :] = jnp.concatenate([s_re, s_im], axis=1).astype(BF16)
        e_re = e[r0:r0 + bsz, 0:ns]
        e_im = e[r0:r0 + bsz, ns:2 * ns]
        s_re, s_im = aqr * s_re - aqi * s_im + e_re, aqr * s_im + aqi * s_re + e_im
    carry_ref[...] = jnp.concatenate([s_re, s_im], axis=1)
    tw = 2 * LANES
    for jt in range(q * LANES // tw):
        hi = (jt + 1) * tw
        y = jnp.dot(xcat[:, 0:hi], t_ref[0:hi, jt * tw:hi], preferred_element_type=F32)
        y = y + jnp.dot(sprev_ref[...], c_ref[:, jt * tw:hi], preferred_element_type=F32)
        for j in range(jt * tw // LANES, hi // LANES):
            yj = y[:, j * LANES - jt * tw:(j + 1) * LANES - jt * tw]
            y_ref[:, j] = yj.reshape(nb, bsz, LANES).astype(BF16)


def _s5_matrices(lam_re, lam_im, b_re, b_im, c_re, c_im, d, log_step, q):
    g, p = lam_re.shape
    cch = b_re.shape[-1]
    gpl = LANES // cch
    nblk = g // gpl
    dt = jnp.exp(log_step)[:, None]
    zr, zi = lam_re * dt, lam_im * dt
    ks = jnp.arange(q + 1, dtype=F32)[:, None, None]
    mag = jnp.exp(ks * zr[None])
    pr, pi = mag * jnp.cos(ks * zi[None]), mag * jnp.sin(ks * zi[None])
    nr, ni = pr[1] - 1.0, pi[1]
    den = lam_re * lam_re + lam_im * lam_im
    fr = (nr * lam_re + ni * lam_im) / den
    fi = (ni * lam_re - nr * lam_im) / den
    hp = gpl * p
    lanes = lambda a: a.reshape(a.shape[:-2] + (nblk, hp))
    prb, pib, frb, fib = lanes(pr), lanes(pi), lanes(fr), lanes(fi)
    b_t = lambda b: b.reshape(nblk, gpl, p, cch).transpose(0, 3, 1, 2).reshape(nblk, cch, hp)
    c_t = lambda c: c.reshape(nblk, gpl, cch, p).transpose(0, 3, 1, 2)
    btr, bti = b_t(b_re), b_t(b_im)
    bbr = frb[:, None] * btr - fib[:, None] * bti
    bbi = frb[:, None] * bti + fib[:, None] * btr
    pre, pie = prb[q - 1::-1], pib[q - 1::-1]
    er = pre[:, :, None] * bbr[None] - pie[:, :, None] * bbi[None]
    ei = pre[:, :, None] * bbi[None] + pie[:, :, None] * bbr[None]
    at = jnp.concatenate([er, ei], axis=3).transpose(1, 0, 2, 3)
    cre, cim = c_re.reshape(nblk, gpl, cch, p), c_im.reshape(nblk, gpl, cch, p)
    er5, ei5 = er.reshape(q, nblk, cch, gpl, p), ei.reshape(q, nblk, cch, gpl, p)
    kern = (jnp.einsum('mhop,imchp->micho', cre, er5, precision=HI)
            - jnp.einsum('mhop,imchp->micho', cim, ei5, precision=HI))
    kt = kern[:, ::-1].reshape(nblk, q, cch, LANES)
    skip = d.reshape(nblk, gpl, cch).transpose(0, 2, 1)[..., None] * jnp.eye(cch, dtype=F32)[None, :, None, :]
    kt = kt.at[:, 0].add(skip.reshape(nblk, cch, LANES))
    crt, cit = c_t(c_re), c_t(c_im)
    prt = pr[1:].reshape(q, nblk, gpl, p).transpose(1, 3, 0, 2)[..., None]
    pit = pi[1:].reshape(q, nblk, gpl, p).transpose(1, 3, 0, 2)[..., None]
    cr = (crt[:, :, None] * prt - cit[:, :, None] * pit).reshape(nblk, p, q * LANES)
    ci = (crt[:, :, None] * pit + cit[:, :, None] * prt).reshape(nblk, p, q * LANES)
    ct = jnp.stack([cr, -ci], axis=1)
    aq = jnp.concatenate([prb[q], pib[q]], axis=1)
    return kt, at, ct, aq


def _s5_scan(u, mats):
    kt, at, ct, aq = mats
    bsz, seq, w = u.shape
    q, nb = S5_Q, S5_NB
    nblk = w // LANES
    ns2 = aq.shape[-1]
    u4 = jnp.transpose(u, (1, 0, 2)).reshape(seq // q, q, bsz, w)
    aq_b = jnp.broadcast_to(aq[:, None, :], (nblk, bsz, ns2))
    blk = pl.BlockSpec((nb, q, bsz, LANES), lambda m, i: (i, 0, 0, m))
    tab = lambda a: pl.BlockSpec((1,) + a.shape[1:], lambda m, i: (m,) + (0,) * (a.ndim - 1))
    y4 = pl.pallas_call(
        _s5_kernel,
        grid=(nblk, seq // (q * nb)),
        in_specs=[blk, tab(kt), tab(at), tab(ct), tab(aq_b)],
        out_specs=blk,
        out_shape=jax.ShapeDtypeStruct(u4.shape, BF16),
        scratch_shapes=[pltpu.VMEM((q * LANES, q * LANES), BF16),
                        pltpu.VMEM((q * LANES, ns2), BF16),
                        pltpu.VMEM((ns2, q * LANES), BF16),
                        pltpu.VMEM((nb * bsz, ns2), BF16),
                        pltpu.VMEM((bsz, ns2), F32)],
        compiler_params=pltpu.CompilerParams(
            dimension_semantics=("arbitrary", "arbitrary"), vmem_limit_bytes=VMEM_LIMIT_BYTES),
        name="s5_scan",
    )(u4, kt, at, ct, aq_b)
    return jnp.transpose(y4.reshape(seq, bsz, w), (1, 0, 2))


def _ffn_kernel(alpha, x_ref, ys_ref, yr_ref, gluw_ref, glub_ref, s5g_ref, wout_ref, ln1g_ref, ln1b_ref,
                wup_ref, cw_ref, cb_ref, wdn_ref, ln2g_ref, ln2b_ref, o_ref, hmid_ref, tail_ref):
    @pl.when(pl.program_id(1) == 0)
    def _():
        tail_ref[...] = jnp.zeros_like(tail_ref)

    s5w = ys_ref.shape[-1]
    dff = hmid_ref.shape[1]
    halo = tail_ref.shape[0]

    def mix(r0, ts):
        y = ys_ref[0, r0:r0 + ts, :].astype(F32)
        cdf = 0.5 * (1.0 + jnp.tanh(math.sqrt(2.0 / math.pi) * (y + 0.044715 * (y * y * y))))
        y = y * cdf
        z = jnp.dot(y.astype(BF16), gluw_ref[...], preferred_element_type=F32) + glub_ref[...]
        y = y * _sigmoid(z)
        y = y * lax.rsqrt(jnp.mean(y * y, axis=-1, keepdims=True) + LN_EPS) * s5g_ref[...]
        mix = jnp.dot(y.astype(BF16), wout_ref[0:s5w, :], preferred_element_type=F32)
        mix = mix + jnp.dot(yr_ref[0, r0:r0 + ts, :], wout_ref[s5w:, :], preferred_element_type=F32)
        return _layer_norm(alpha * x_ref[0, r0:r0 + ts, :] + mix, ln1g_ref[...], ln1b_ref[...])

    def chunk(r0, xb, c):
        ts = xb.shape[0]
        c0 = c * FC_FFN
        c1 = min(c0 + FC_FFN, dff)
        a = jnp.dot(xb, wup_ref[:, c0:c1], preferred_element_type=F32)
        g = jnp.dot(xb, wup_ref[:, dff + c0:dff + c1], preferred_element_type=F32)
        ext = jnp.concatenate([tail_ref[:, c0:c1], a], axis=0)
        tail_ref[:, c0:c1] = a[ts - halo:ts]
        a1 = ext[halo - 1:halo - 1 + ts]
        a2 = ext[halo - 2:halo - 2 + ts]
        cv = cw_ref[2:3, c0:c1] * a + cw_ref[1:2, c0:c1] * a1 + cw_ref[0:1, c0:c1] * a2 + cb_ref[:, c0:c1]
        hmid_ref[r0:r0 + ts, c0:c1] = (cv * _sigmoid(cv) * g).astype(BF16)

    def out(r0, res):
        o_ref[0, r0:r0 + res.shape[0], :] = _layer_norm(res, ln2g_ref[...], ln2b_ref[...])

    starts = [sum(FFN_SUBS[:i]) for i in range(len(FFN_SUBS))]
    x1 = mix(0, FFN_SUBS[0])
    res_prev = None
    for sub, (r0, ts) in enumerate(zip(starts, FFN_SUBS)):
        xb = x1.astype(BF16)
        x1_next = None
        for c in range(pl.cdiv(dff, FC_FFN)):
            chunk(r0, xb, c)
            if c == FFN_MIX_AT and sub + 1 < len(FFN_SUBS):
                x1_next = mix(r0 + ts, FFN_SUBS[sub + 1])
            if c == FFN_OUT_AT and res_prev is not None:
                out(starts[sub - 1], res_prev)
        res_prev = alpha * x1 + jnp.dot(hmid_ref[r0:r0 + ts, :], wdn_ref[...], preferred_element_type=F32)
        x1 = x1_next
    out(starts[-1], res_prev)


def _mixer_ffn(alpha, x, ys, yr, glu_w, glu_b, s5_gain, w_out, ln1_g, ln1_b,
               w_up, conv_w, conv_b, w_down, ln2_g, ln2_b):
    bsz, seq, d = x.shape
    tm = TM_FFN
    assert sum(FFN_SUBS) == tm and max(FFN_MIX_AT, FFN_OUT_AT) < pl.cdiv(w_down.shape[0], FC_FFN)
    s5w, ret_w = ys.shape[-1], yr.shape[-1]
    dff = w_down.shape[0]
    row = lambda a: a.reshape(1, -1).astype(F32)
    tok = lambda w: pl.BlockSpec((1, tm, w), lambda b, i: (b, i, 0))
    assert all(w.dtype == BF16 for w in (glu_w, w_out, w_up, w_down))
    operands = [x, ys, yr, glu_w, row(glu_b), row(s5_gain), w_out, row(ln1_g), row(ln1_b), w_up,
                conv_w.astype(F32), row(conv_b), w_down, row(ln2_g), row(ln2_b)]
    in_specs = [tok(d), tok(s5w), tok(ret_w)] + [_const_spec(a.shape) for a in operands[3:]]
    return pl.pallas_call(
        functools.partial(_ffn_kernel, alpha),
        grid=(bsz, seq // tm),
        in_specs=in_specs,
        out_specs=tok(d),
        out_shape=jax.ShapeDtypeStruct((bsz, seq, d), x.dtype),
        scratch_shapes=[pltpu.VMEM((tm, dff), BF16), pltpu.VMEM((8, dff), F32)],
        compiler_params=pltpu.CompilerParams(
            dimension_semantics=("arbitrary", "arbitrary"), vmem_limit_bytes=VMEM_LIMIT_BYTES),
        name="mixer_ffn",
    )(*operands)


def kernel(x, w_in, s5_lambda_re, s5_lambda_im, s5_b_re, s5_b_im, s5_c_re, s5_c_im, s5_d, s5_log_step,
           s5_glu_w, s5_glu_b, s5_out_gain, ret_gn_gain, w_out, ln1_g, ln1_b, ffn_w_up, ffn_conv_w,
           ffn_conv_b, ffn_w_down, ln2_g, ln2_b):
    depth = w_in.shape[0]
    alpha = (2.0 * depth) ** 0.25
    s5w = s5_d.shape[-1]
    for l in range(depth):
        u, yr, glu_w, w_o, w_up, w_down = _mixer_front(
            x, w_in[l], ret_gn_gain[l], s5w, [s5_glu_w[l], w_out[l], ffn_w_up[l], ffn_w_down[l]])
        s5_params = (s5_lambda_re[l], s5_lambda_im[l], s5_b_re[l], s5_b_im[l], s5_c_re[l], s5_c_im[l],
                     s5_d[l], s5_log_step[l])
        u, s5_params = lax.optimization_barrier((u, s5_params))
        mats = _s5_matrices(*s5_params, S5_Q)
        ys = _s5_scan(u, mats)
        x = _mixer_ffn(alpha, x, ys, yr, glu_w, s5_glu_b[l], s5_out_gain[l], w_o, ln1_g[l],
                       ln1_b[l], w_up, ffn_conv_w[l], ffn_conv_b[l], w_down, ln2_g[l], ln2_b[l])
    return x
```

```python
import functools
import math

import numpy as np
import jax
import jax.numpy as jnp
from jax import lax
from jax.experimental import pallas as pl
from jax.experimental.pallas import tpu as pltpu

CHUNK = 64
S5_GROUP = 16
S5_STATE = 64
RET_HEADS = 4
RET_DK = 64
RET_DV = 128
CONV_W = 3
ROPE_BASE = 10000.0
LN_EPS = 1e-5

LANES = 128
SUBLANES_BF16 = 16
VMEM_LIMIT_BYTES = 56 * 1024 * 1024

TL_FRONT = 2048
TB_RET = 256
S5_Q = 8
S5_NB = 64
TM_FFN = 512
FC_FFN = 256
FFN_SUBS = (256, 256)
FFN_MIX_AT = 1
FFN_OUT_AT = 6

F32 = jnp.float32
BF16 = jnp.bfloat16
HI = lax.Precision.HIGHEST


def _const_spec(shape):
    nd = len(shape)
    return pl.BlockSpec(shape, lambda *_: (0,) * nd, pipeline_mode=pl.Buffered(1))


def _layer_norm(v, g, b):
    mu = jnp.mean(v, axis=-1, keepdims=True)
    vc = v - mu
    var = jnp.mean(vc * vc, axis=-1, keepdims=True)
    return vc * lax.rsqrt(var + LN_EPS) * g + b


def _sigmoid(v):
    return 1.0 / (1.0 + jnp.exp(-v))


def _front_kernel(n_cast, x_ref, w_ref, cos_ref, sin_ref, hm_ref, dmask_ref, xi_ref, zk_ref,
                  gdec_ref, smask_ref, gn_ref, *rest):
    cast_in = rest[:n_cast]
    u_ref, yr_ref = rest[n_cast:n_cast + 2]
    cast_out = rest[n_cast + 2:2 * n_cast + 2]
    state_ref = rest[2 * n_cast + 2]
    @pl.when(pl.program_id(1) == 0)
    def _():
        state_ref[...] = jnp.zeros_like(state_ref)

    s5w = u_ref.shape[-1]
    xb = x_ref[0].astype(BF16)
    o_qk = s5w
    o_v = o_qk + 4 * LANES
    ret_w = RET_HEADS * RET_DV
    pw = 2 * RET_DV

    def project(r0):
        xr = xb[r0:r0 + TB_RET]
        qk = jnp.dot(xr, w_ref[:, o_qk:o_qk + 4 * LANES], preferred_element_type=F32)
        cos = cos_ref[r0:r0 + TB_RET, :]
        sin = sin_ref[r0:r0 + TB_RET, :]
        q1, q2 = qk[:, 0:LANES], qk[:, LANES:2 * LANES]
        k1, k2 = qk[:, 2 * LANES:3 * LANES], qk[:, 3 * LANES:4 * LANES]
        qb = jnp.concatenate([q1 * cos - q2 * sin, q1 * sin + q2 * cos], axis=1).astype(BF16)
        kb = jnp.concatenate([k1 * cos - k2 * sin, k1 * sin + k2 * cos], axis=1)
        v = jnp.dot(xr, w_ref[:, o_v:o_v + ret_w], preferred_element_type=F32).astype(BF16)
        gate = jnp.dot(xr, w_ref[:, o_v + ret_w:o_v + 2 * ret_w], preferred_element_type=F32)
        return qb, kb, v, gate

    def retain(r0, qb, kb, v, gate):
        kbb = kb.astype(BF16)
        for pair in range(RET_HEADS // 2):
            p0 = pair * pw
            st = state_ref[pair]
            cross = xi_ref[pair] * jnp.dot(qb, st.astype(BF16), preferred_element_type=F32)
            kz = (kb * zk_ref[pair]).astype(BF16)
            upd = lax.dot_general(kz, v[:, p0:p0 + pw], (((0,), (0,)), ((), ())), preferred_element_type=F32)
            state_ref[pair] = st * gdec_ref[pair] + upd * smask_ref[pair]
            for sub in range(2):
                hd = 2 * pair + sub
                c0 = hd * RET_DV
                kh = kbb * hm_ref[hd]
                sc = lax.dot_general(qb, kh, (((1,), (1,)), ((), ())), preferred_element_type=F32)
                p = (sc * dmask_ref[hd]).astype(BF16)
                o = jnp.dot(p, v[:, c0:c0 + RET_DV], preferred_element_type=F32)
                o = o + cross[:, sub * RET_DV:(sub + 1) * RET_DV]
                mu = jnp.mean(o, axis=-1, keepdims=True)
                oc = o - mu
                var = jnp.mean(oc * oc, axis=-1, keepdims=True)
                on = oc * lax.rsqrt(var + LN_EPS) * gn_ref[:, c0:c0 + RET_DV]
                g = gate[:, c0:c0 + RET_DV]
                yr_ref[0, r0:r0 + TB_RET, c0:c0 + RET_DV] = (on * (g * _sigmoid(g))).astype(BF16)

    n_blk = xb.shape[0] // TB_RET
    cur = project(0)
    for blk in range(n_blk):
        nxt = project((blk + 1) * TB_RET) if blk + 1 < n_blk else None
        retain(blk * TB_RET, *cur)
        cur = nxt

    u_ref[0] = jnp.dot(xb, w_ref[:, 0:s5w], preferred_element_type=F32).astype(BF16)

    for src, dst in zip(cast_in, cast_out):
        dst[...] = src[...].astype(BF16)


def _retention_consts(tb):
    log_gamma = np.log1p(-(2.0 ** (-5.0 - np.arange(RET_HEADS, dtype=np.float64))))
    pos = np.arange(tb, dtype=np.float64)
    chunk = np.arange(tb) // CHUNK
    visible = (chunk[None, :] <= chunk[:, None]).astype(np.float64)
    dist = np.abs(pos[:, None] - pos[None, :])
    dmask = np.exp(dist[None] * log_gamma[:, None, None]) * visible[None]
    xi = np.exp((pos + 1.0)[None, :] * log_gamma[:, None])
    zeta = np.exp((tb - 1.0 - pos)[None, :] * log_gamma[:, None])
    gdec = np.exp(tb * log_gamma)
    lane = np.arange(2 * LANES) % LANES
    half = RET_DK // 2
    hm = np.stack([(lane // half == h) for h in range(RET_HEADS)]).astype(np.float64) * RET_DK ** -0.5
    npair = RET_HEADS // 2
    own = (hm > 0).astype(np.float64)
    xi_p = np.repeat(xi[:, :, None], RET_DV, axis=2).reshape(npair, 2, tb, RET_DV)
    xi_p = np.concatenate([xi_p[:, 0], xi_p[:, 1]], axis=2)
    zk = (zeta[:, :, None] * hm[:, None, :]).reshape(npair, 2, tb, 2 * LANES).sum(axis=1)
    col_head = np.repeat(np.arange(RET_HEADS).reshape(npair, 2), RET_DV, axis=1)
    smask = np.stack([own[col_head[p]].T for p in range(npair)])
    gdec_p = smask * gdec[col_head][:, None, :]
    tables = tuple(jnp.asarray(np.ascontiguousarray(a), F32) for a in (dmask, xi_p, zk, gdec_p, smask))
    return (jnp.asarray(hm[:, None, :], BF16),) + tables


def _front_weight(w_in, s5w):
    half = RET_DK // 2
    nqk = RET_HEADS * RET_DK
    w = w_in.astype(BF16)

    def regroup(cols):
        return cols.reshape(-1, RET_HEADS, 2, half).transpose(0, 2, 1, 3).reshape(-1, nqk)

    return jnp.concatenate([w[:, :s5w], regroup(w[:, s5w:s5w + nqk]),
                            regroup(w[:, s5w + nqk:s5w + 2 * nqk]), w[:, s5w + 2 * nqk:]], axis=1)


def _mixer_front(x, w_in, gn_gain, s5w, later_weights):
    bsz, seq, d = x.shape
    tl = TL_FRONT
    ret_w = RET_HEADS * RET_DV
    half = RET_DK // 2
    freqs = np.float32(ROPE_BASE) ** (-np.arange(half, dtype=np.float32) / np.float32(half))
    ang = (np.arange(seq, dtype=np.float32)[:, None] * freqs[None, :]).astype(np.float64)
    cos = jnp.asarray(np.tile(np.cos(ang), (1, RET_HEADS)), F32)
    sin = jnp.asarray(np.tile(np.sin(ang), (1, RET_HEADS)), F32)
    hm, dmask, xi_p, zk, gdec_p, smask = _retention_consts(TB_RET)
    w = _front_weight(w_in, s5w)
    rows = lambda a: pl.BlockSpec((a.shape[0] // bsz, a.shape[1]), lambda b, i: (b, 0))
    assert all(a.shape[0] % (bsz * SUBLANES_BF16) == 0 for a in later_weights)
    return pl.pallas_call(
        functools.partial(_front_kernel, len(later_weights)),
        grid=(bsz, seq // tl),
        in_specs=[
            pl.BlockSpec((1, tl, d), lambda b, i: (b, i, 0)),
            _const_spec(w.shape),
            pl.BlockSpec((tl, LANES), lambda b, i: (i, 0)),
            pl.BlockSpec((tl, LANES), lambda b, i: (i, 0)),
            _const_spec(hm.shape), _const_spec(dmask.shape), _const_spec(xi_p.shape),
            _const_spec(zk.shape), _const_spec(gdec_p.shape), _const_spec(smask.shape), _const_spec((1, ret_w)),
        ] + [rows(a) for a in later_weights],
        out_specs=[
            pl.BlockSpec((1, tl, s5w), lambda b, i: (b, i, 0)),
            pl.BlockSpec((1, tl, ret_w), lambda b, i: (b, i, 0)),
        ] + [rows(a) for a in later_weights],
        out_shape=[jax.ShapeDtypeStruct((bsz, seq, s5w), BF16),
                   jax.ShapeDtypeStruct((bsz, seq, ret_w), BF16)]
                  + [jax.ShapeDtypeStruct(a.shape, BF16) for a in later_weights],
        scratch_shapes=[pltpu.VMEM((RET_HEADS // 2, 2 * LANES, 2 * RET_DV), F32)],
        compiler_params=pltpu.CompilerParams(
            dimension_semantics=("arbitrary", "arbitrary"), vmem_limit_bytes=VMEM_LIMIT_BYTES),
        name="mixer_front",
    )(x, w, cos, sin, hm, dmask, xi_p, zk, gdec_p, smask, gn_gain.reshape(1, ret_w).astype(F32), *later_weights)


def _same_group(shape, row_shift, col_shift):
    gpl_mask = LANES // S5_GROUP - 1
    rows = lax.broadcasted_iota(jnp.int32, shape, 0) >> row_shift
    cols = (lax.broadcasted_iota(jnp.int32, shape, 1) >> col_shift) & gpl_mask
    return (rows == cols).astype(F32)


def _s5_build(kt_ref, at_ref, ct_ref, t_ref, e_ref, c_ref):
    q = kt_ref.shape[1]
    gpl = LANES // S5_GROUP
    ch_shift = S5_GROUP.bit_length() - 1
    st_shift = S5_STATE.bit_length() - 1
    t_ref[...] = jnp.zeros_like(t_ref)
    mask_t = _same_group((LANES, LANES), ch_shift, ch_shift)
    for lag in range(q):
        bd = (jnp.concatenate([kt_ref[0, lag]] * gpl, axis=0) * mask_t).astype(BF16)
        for i in range(q - lag):
            j = i + lag
            t_ref[i * LANES:(i + 1) * LANES, j * LANES:(j + 1) * LANES] = bd
    mask_e = _same_group((LANES, e_ref.shape[1]), ch_shift, st_shift)
    for i in range(q):
        e_ref[i * LANES:(i + 1) * LANES, :] = (jnp.concatenate([at_ref[0, i]] * gpl, axis=0) * mask_e).astype(BF16)
    ns = c_ref.shape[0] // 2
    mask_c = _same_group((ns, c_ref.shape[1]), st_shift, ch_shift)
    for part in range(2):
        c_ref[part * ns:(part + 1) * ns, :] = (jnp.concatenate([ct_ref[0, part]] * gpl, axis=0) * mask_c).astype(BF16)


def _s5_kernel(u_ref, kt_ref, at_ref, ct_ref, aq_ref, y_ref, t_ref, e_ref, c_ref, sprev_ref, carry_ref):
    @pl.when(pl.program_id(1) == 0)
    def _():
        carry_ref[...] = jnp.zeros_like(carry_ref)
        _s5_build(kt_ref, at_ref, ct_ref, t_ref, e_ref, c_ref)

    nb, q, bsz, _ = u_ref.shape
    xcat = jnp.concatenate([u_ref[:, j].reshape(nb * bsz, LANES) for j in range(q)], axis=1)
    e = jnp.dot(xcat, e_ref[...], preferred_element_type=F32)
    ns = e.shape[1] // 2
    aqr = aq_ref[0, :, 0:ns]
    aqi = aq_ref[0, :, ns:2 * ns]
    s_re = carry_ref[:, 0:ns]
    s_im = carry_ref[:, ns:2 * ns]
    for n in range(nb):
        r0 = n * bsz
        sprev_ref[r0:r0 + bsz, :] = jnp.concatenate([s_re, s_im], axis=1).astype(BF16)
        e_re = e[r0:r0 + bsz, 0:ns]
        e_im = e[r0:r0 + bsz, ns:2 * ns]
        s_re, s_im = aqr * s_re - aqi * s_im + e_re, aqr * s_im + aqi * s_re + e_im
    carry_ref[...] = jnp.concatenate([s_re, s_im], axis=1)
    tw = 2 * LANES
    for jt in range(q * LANES // tw):
        hi = (jt + 1) * tw
        y = jnp.dot(xcat[:, 0:hi], t_ref[0:hi, jt * tw:hi], preferred_element_type=F32)
        y = y + jnp.dot(sprev_ref[...], c_ref[:, jt * tw:hi], preferred_element_type=F32)
        for j in range(jt * tw // LANES, hi // LANES):
            yj = y[:, j * LANES - jt * tw:(j + 1) * LANES - jt * tw]
            y_ref[:, j] = yj.reshape(nb, bsz, LANES).astype(BF16)


def _s5_matrices(lam_re, lam_im, b_re, b_im, c_re, c_im, d, log_step, q):
    g, p = lam_re.shape
    cch = b_re.shape[-1]
    gpl = LANES // cch
    nblk = g // gpl
    dt = jnp.exp(log_step)[:, None]
    zr, zi = lam_re * dt, lam_im * dt
    ks = jnp.arange(q + 1, dtype=F32)[:, None, None]
    mag = jnp.exp(ks * zr[None])
    pr, pi = mag * jnp.cos(ks * zi[None]), mag * jnp.sin(ks * zi[None])
    nr, ni = pr[1] - 1.0, pi[1]
    den = lam_re * lam_re + lam_im * lam_im
    fr = (nr * lam_re + ni * lam_im) / den
    fi = (ni * lam_re - nr * lam_im) / den
    hp = gpl * p
    lanes = lambda a: a.reshape(a.shape[:-2] + (nblk, hp))
    prb, pib, frb, fib = lanes(pr), lanes(pi), lanes(fr), lanes(fi)
    b_t = lambda b: b.reshape(nblk, gpl, p, cch).transpose(0, 3, 1, 2).reshape(nblk, cch, hp)
    c_t = lambda c: c.reshape(nblk, gpl, cch, p).transpose(0, 3, 1, 2)
    btr, bti = b_t(b_re), b_t(b_im)
    bbr = frb[:, None] * btr - fib[:, None] * bti
    bbi = frb[:, None] * bti + fib[:, None] * btr
    pre, pie = prb[q - 1::-1], pib[q - 1::-1]
    er = pre[:, :, None] * bbr[None] - pie[:, :, None] * bbi[None]
    ei = pre[:, :, None] * bbi[None] + pie[:, :, None] * bbr[None]
    at = jnp.concatenate([er, ei], axis=3).transpose(1, 0, 2, 3)
    cre, cim = c_re.reshape(nblk, gpl, cch, p), c_im.reshape(nblk, gpl, cch, p)
    er5, ei5 = er.reshape(q, nblk, cch, gpl, p), ei.reshape(q, nblk, cch, gpl, p)
    kern = (jnp.einsum('mhop,imchp->micho', cre, er5, precision=HI)
            - jnp.einsum('mhop,imchp->micho', cim, ei5, precision=HI))
    kt = kern[:, ::-1].reshape(nblk, q, cch, LANES)
    skip = d.reshape(nblk, gpl, cch).transpose(0, 2, 1)[..., None] * jnp.eye(cch, dtype=F32)[None, :, None, :]
    kt = kt.at[:, 0].add(skip.reshape(nblk, cch, LANES))
    crt, cit = c_t(c_re), c_t(c_im)
    prt = pr[1:].reshape(q, nblk, gpl, p).transpose(1, 3, 0, 2)[..., None]
    pit = pi[1:].reshape(q, nblk, gpl, p).transpose(1, 3, 0, 2)[..., None]
    cr = (crt[:, :, None] * prt - cit[:, :, None] * pit).reshape(nblk, p, q * LANES)
    ci = (crt[:, :, None] * pit + cit[:, :, None] * prt).reshape(nblk, p, q * LANES)
    ct = jnp.stack([cr, -ci], axis=1)
    aq = jnp.concatenate([prb[q], pib[q]], axis=1)
    return kt, at, ct, aq


def _s5_scan(u, mats):
    kt, at, ct, aq = mats
    bsz, seq, w = u.shape
    q, nb = S5_Q, S5_NB
    nblk = w // LANES
    ns2 = aq.shape[-1]
    u4 = jnp.transpose(u, (1, 0, 2)).reshape(seq // q, q, bsz, w)
    aq_b = jnp.broadcast_to(aq[:, None, :], (nblk, bsz, ns2))
    blk = pl.BlockSpec((nb, q, bsz, LANES), lambda m, i: (i, 0, 0, m))
    tab = lambda a: pl.BlockSpec((1,) + a.shape[1:], lambda m, i: (m,) + (0,) * (a.ndim - 1))
    y4 = pl.pallas_call(
        _s5_kernel,
        grid=(nblk, seq // (q * nb)),
        in_specs=[blk, tab(kt), tab(at), tab(ct), tab(aq_b)],
        out_specs=blk,
        out_shape=jax.ShapeDtypeStruct(u4.shape, BF16),
        scratch_shapes=[pltpu.VMEM((q * LANES, q * LANES), BF16),
                        pltpu.VMEM((q * LANES, ns2), BF16),
                        pltpu.VMEM((ns2, q * LANES), BF16),
                        pltpu.VMEM((nb * bsz, ns2), BF16),
                        pltpu.VMEM((bsz, ns2), F32)],
        compiler_params=pltpu.CompilerParams(
            dimension_semantics=("arbitrary", "arbitrary"), vmem_limit_bytes=VMEM_LIMIT_BYTES),
        name="s5_scan",
    )(u4, kt, at, ct, aq_b)
    return jnp.transpose(y4.reshape(seq, bsz, w), (1, 0, 2))


def _ffn_kernel(alpha, x_ref, ys_ref, yr_ref, gluw_ref, glub_ref, s5g_ref, wout_ref, ln1g_ref, ln1b_ref,
                wup_ref, cw_ref, cb_ref, wdn_ref, ln2g_ref, ln2b_ref, o_ref, xb_ref, hmid_ref, tail_ref):
    @pl.when(pl.program_id(1) == 0)
    def _():
        tail_ref[...] = jnp.zeros_like(tail_ref)

    s5w = ys_ref.shape[-1]
    dff = hmid_ref.shape[1]
    halo = tail_ref.shape[0]

    def mix(r0, ts):
        y = ys_ref[0, r0:r0 + ts, :].astype(F32)
        cdf = 0.5 * (1.0 + jnp.tanh(math.sqrt(2.0 / math.pi) * (y + 0.044715 * (y * y * y))))
        y = y * cdf
        z = jnp.dot(y.astype(BF16), gluw_ref[...], preferred_element_type=F32) + glub_ref[...]
        y = y * _sigmoid(z)
        y = y * lax.rsqrt(jnp.mean(y * y, axis=-1, keepdims=True) + LN_EPS) * s5g_ref[...]
        mix = jnp.dot(y.astype(BF16), wout_ref[0:s5w, :], preferred_element_type=F32)
        mix = mix + jnp.dot(yr_ref[0, r0:r0 + ts, :], wout_ref[s5w:, :], preferred_element_type=F32)
        x1 = _layer_norm(alpha * x_ref[0, r0:r0 + ts, :] + mix, ln1g_ref[...], ln1b_ref[...])
        o_ref[0, r0:r0 + ts, :] = x1
        xb_ref[r0:r0 + ts, :] = x1.astype(BF16)

    def chunk(r0, ts, c):
        xb = xb_ref[r0:r0 + ts, :]
        c0 = c * FC_FFN
        c1 = min(c0 + FC_FFN, dff)
        a = jnp.dot(xb, wup_ref[:, c0:c1], preferred_element_type=F32)
        g = jnp.dot(xb, wup_ref[:, dff + c0:dff + c1], preferred_element_type=F32)
        ext = jnp.concatenate([tail_ref[:, c0:c1], a], axis=0)
        tail_ref[:, c0:c1] = a[ts - halo:ts]
        a1 = ext[halo - 1:halo - 1 + ts]
        a2 = ext[halo - 2:halo - 2 + ts]
        cv = cw_ref[2:3, c0:c1] * a + cw_ref[1:2, c0:c1] * a1 + cw_ref[0:1, c0:c1] * a2 + cb_ref[:, c0:c1]
        hmid_ref[r0:r0 + ts, c0:c1] = (cv * _sigmoid(cv) * g).astype(BF16)

    def down(r0, ts):
        ffn = jnp.dot(hmid_ref[r0:r0 + ts, :], wdn_ref[...], preferred_element_type=F32)
        o_ref[0, r0:r0 + ts, :] = alpha * o_ref[0, r0:r0 + ts, :] + ffn

    def out(r0, ts):
        o_ref[0, r0:r0 + ts, :] = _layer_norm(o_ref[0, r0:r0 + ts, :], ln2g_ref[...], ln2b_ref[...])

    starts = [sum(FFN_SUBS[:i]) for i in range(len(FFN_SUBS))]
    mix(0, FFN_SUBS[0])
    for sub, (r0, ts) in enumerate(zip(starts, FFN_SUBS)):
        for c in range(pl.cdiv(dff, FC_FFN)):
            chunk(r0, ts, c)
            if c == FFN_MIX_AT and sub + 1 < len(FFN_SUBS):
                mix(r0 + ts, FFN_SUBS[sub + 1])
            if c == FFN_OUT_AT and sub > 0:
                out(starts[sub - 1], FFN_SUBS[sub - 1])
        down(r0, ts)
    out(starts[-1], FFN_SUBS[-1])


def _mixer_ffn(alpha, x, ys, yr, glu_w, glu_b, s5_gain, w_out, ln1_g, ln1_b,
               w_up, conv_w, conv_b, w_down, ln2_g, ln2_b):
    bsz, seq, d = x.shape
    tm = TM_FFN
    assert sum(FFN_SUBS) == tm and max(FFN_MIX_AT, FFN_OUT_AT) < pl.cdiv(w_down.shape[0], FC_FFN)
    s5w, ret_w = ys.shape[-1], yr.shape[-1]
    dff = w_down.shape[0]
    row = lambda a: a.reshape(1, -1).astype(F32)
    tok = lambda w: pl.BlockSpec((1, tm, w), lambda b, i: (b, i, 0))
    assert all(w.dtype == BF16 for w in (glu_w, w_out, w_up, w_down))
    operands = [x, ys, yr, glu_w, row(glu_b), row(s5_gain), w_out, row(ln1_g), row(ln1_b), w_up,
                conv_w.astype(F32), row(conv_b), w_down, row(ln2_g), row(ln2_b)]
    in_specs = [tok(d), tok(s5w), tok(ret_w)] + [_const_spec(a.shape) for a in operands[3:]]
    return pl.pallas_call(
        functools.partial(_ffn_kernel, alpha),
        grid=(bsz, seq // tm),
        in_specs=in_specs,
        out_specs=tok(d),
        out_shape=jax.ShapeDtypeStruct((bsz, seq, d), x.dtype),
        scratch_shapes=[pltpu.VMEM((tm, d), BF16), pltpu.VMEM((tm, dff), BF16), pltpu.VMEM((8, dff), F32)],
        compiler_params=pltpu.CompilerParams(
            dimension_semantics=("arbitrary", "arbitrary"), vmem_limit_bytes=VMEM_LIMIT_BYTES),
        name="mixer_ffn",
    )(*operands)


def kernel(x, w_in, s5_lambda_re, s5_lambda_im, s5_b_re, s5_b_im, s5_c_re, s5_c_im, s5_d, s5_log_step,
           s5_glu_w, s5_glu_b, s5_out_gain, ret_gn_gain, w_out, ln1_g, ln1_b, ffn_w_up, ffn_conv_w,
           ffn_conv_b, ffn_w_down, ln2_g, ln2_b):
    depth = w_in.shape[0]
    alpha = (2.0 * depth) ** 0.25
    s5w = s5_d.shape[-1]
    for l in range(depth):
        u, yr, glu_w, w_o, w_up, w_down = _mixer_front(
            x, w_in[l], ret_gn_gain[l], s5w, [s5_glu_w[l], w_out[l], ffn_w_up[l], ffn_w_down[l]])
        mats = _s5_matrices(s5_lambda_re[l], s5_lambda_im[l], s5_b_re[l], s5_b_im[l], s5_c_re[l],
                            s5_c_im[l], s5_d[l], s5_log_step[l], S5_Q)
        ys = _s5_scan(u, mats)
        x = _mixer_ffn(alpha, x, ys, yr, glu_w, s5_glu_b[l], s5_out_gain[l], w_o, ln1_g[l],
                       ln1_b[l], w_up, ffn_conv_w[l], ffn_conv_b[l], w_down, ln2_g[l], ln2_b[l])
    return x
```

```python
import functools
import math

import numpy as np
import jax
import jax.numpy as jnp
from jax import lax
from jax.experimental import pallas as pl
from jax.experimental.pallas import tpu as pltpu

CHUNK = 64
S5_GROUP = 16
S5_STATE = 64
RET_HEADS = 4
RET_DK = 64
RET_DV = 128
CONV_W = 3
ROPE_BASE = 10000.0
LN_EPS = 1e-5

LANES = 128
SUBLANES_BF16 = 16
VMEM_LIMIT_BYTES = 56 * 1024 * 1024

TL_FRONT = 2048
TB_RET = 256
S5_Q = 8
S5_BW = 64
S5_NB = 64
TM_FFN = 512
FC_FFN = 256
FFN_SUBS = (256, 256)
FFN_MIX_AT = 1
FFN_OUT_AT = 6

F32 = jnp.float32
BF16 = jnp.bfloat16
HI = lax.Precision.HIGHEST


def _const_spec(shape):
    nd = len(shape)
    return pl.BlockSpec(shape, lambda *_: (0,) * nd, pipeline_mode=pl.Buffered(1))


def _layer_norm(v, g, b):
    mu = jnp.mean(v, axis=-1, keepdims=True)
    vc = v - mu
    var = jnp.mean(vc * vc, axis=-1, keepdims=True)
    return vc * lax.rsqrt(var + LN_EPS) * g + b


def _sigmoid(v):
    return 1.0 / (1.0 + jnp.exp(-v))


def _front_kernel(n_cast, x_ref, w_ref, cos_ref, sin_ref, hm_ref, dmask_ref, xi_ref, zk_ref,
                  gdec_ref, smask_ref, gn_ref, *rest):
    cast_in = rest[:n_cast]
    u_ref, yr_ref = rest[n_cast:n_cast + 2]
    cast_out = rest[n_cast + 2:2 * n_cast + 2]
    state_ref = rest[2 * n_cast + 2]
    @pl.when(pl.program_id(1) == 0)
    def _():
        state_ref[...] = jnp.zeros_like(state_ref)

    s5w = u_ref.shape[-1]
    xb = x_ref[0].astype(BF16)
    o_qk = s5w
    o_v = o_qk + 4 * LANES
    ret_w = RET_HEADS * RET_DV
    pw = 2 * RET_DV

    def project(r0):
        xr = xb[r0:r0 + TB_RET]
        qk = jnp.dot(xr, w_ref[:, o_qk:o_qk + 4 * LANES], preferred_element_type=F32)
        cos = cos_ref[r0:r0 + TB_RET, :]
        sin = sin_ref[r0:r0 + TB_RET, :]
        q1, q2 = qk[:, 0:LANES], qk[:, LANES:2 * LANES]
        k1, k2 = qk[:, 2 * LANES:3 * LANES], qk[:, 3 * LANES:4 * LANES]
        qb = jnp.concatenate([q1 * cos - q2 * sin, q1 * sin + q2 * cos], axis=1).astype(BF16)
        kb = jnp.concatenate([k1 * cos - k2 * sin, k1 * sin + k2 * cos], axis=1)
        v = jnp.dot(xr, w_ref[:, o_v:o_v + ret_w], preferred_element_type=F32).astype(BF16)
        gate = jnp.dot(xr, w_ref[:, o_v + ret_w:o_v + 2 * ret_w], preferred_element_type=F32)
        return qb, kb, v, gate

    def retain(r0, qb, kb, v, gate):
        kbb = kb.astype(BF16)
        for pair in range(RET_HEADS // 2):
            p0 = pair * pw
            st = state_ref[pair]
            cross = xi_ref[pair] * jnp.dot(qb, st.astype(BF16), preferred_element_type=F32)
            kz = (kb * zk_ref[pair]).astype(BF16)
            upd = lax.dot_general(kz, v[:, p0:p0 + pw], (((0,), (0,)), ((), ())), preferred_element_type=F32)
            state_ref[pair] = st * gdec_ref[pair] + upd * smask_ref[pair]
            for sub in range(2):
                hd = 2 * pair + sub
                c0 = hd * RET_DV
                kh = kbb * hm_ref[hd]
                sc = lax.dot_general(qb, kh, (((1,), (1,)), ((), ())), preferred_element_type=F32)
                p = (sc * dmask_ref[hd]).astype(BF16)
                o = jnp.dot(p, v[:, c0:c0 + RET_DV], preferred_element_type=F32)
                o = o + cross[:, sub * RET_DV:(sub + 1) * RET_DV]
                mu = jnp.mean(o, axis=-1, keepdims=True)
                oc = o - mu
                var = jnp.mean(oc * oc, axis=-1, keepdims=True)
                on = oc * lax.rsqrt(var + LN_EPS) * gn_ref[:, c0:c0 + RET_DV]
                g = gate[:, c0:c0 + RET_DV]
                yr_ref[0, r0:r0 + TB_RET, c0:c0 + RET_DV] = (on * (g * _sigmoid(g))).astype(BF16)

    n_blk = xb.shape[0] // TB_RET
    cur = project(0)
    for blk in range(n_blk):
        nxt = project((blk + 1) * TB_RET) if blk + 1 < n_blk else None
        retain(blk * TB_RET, *cur)
        cur = nxt

    u_ref[0] = jnp.dot(xb, w_ref[:, 0:s5w], preferred_element_type=F32).astype(BF16)

    for src, dst in zip(cast_in, cast_out):
        dst[...] = src[...].astype(BF16)


def _retention_consts(tb):
    log_gamma = np.log1p(-(2.0 ** (-5.0 - np.arange(RET_HEADS, dtype=np.float64))))
    pos = np.arange(tb, dtype=np.float64)
    chunk = np.arange(tb) // CHUNK
    visible = (chunk[None, :] <= chunk[:, None]).astype(np.float64)
    dist = np.abs(pos[:, None] - pos[None, :])
    dmask = np.exp(dist[None] * log_gamma[:, None, None]) * visible[None]
    xi = np.exp((pos + 1.0)[None, :] * log_gamma[:, None])
    zeta = np.exp((tb - 1.0 - pos)[None, :] * log_gamma[:, None])
    gdec = np.exp(tb * log_gamma)
    lane = np.arange(2 * LANES) % LANES
    half = RET_DK // 2
    hm = np.stack([(lane // half == h) for h in range(RET_HEADS)]).astype(np.float64) * RET_DK ** -0.5
    npair = RET_HEADS // 2
    own = (hm > 0).astype(np.float64)
    xi_p = np.repeat(xi[:, :, None], RET_DV, axis=2).reshape(npair, 2, tb, RET_DV)
    xi_p = np.concatenate([xi_p[:, 0], xi_p[:, 1]], axis=2)
    zk = (zeta[:, :, None] * hm[:, None, :]).reshape(npair, 2, tb, 2 * LANES).sum(axis=1)
    col_head = np.repeat(np.arange(RET_HEADS).reshape(npair, 2), RET_DV, axis=1)
    smask = np.stack([own[col_head[p]].T for p in range(npair)])
    gdec_p = smask * gdec[col_head][:, None, :]
    tables = tuple(jnp.asarray(np.ascontiguousarray(a), F32) for a in (dmask, xi_p, zk, gdec_p, smask))
    return (jnp.asarray(hm[:, None, :], BF16),) + tables


def _front_weight(w_in, s5w):
    half = RET_DK // 2
    nqk = RET_HEADS * RET_DK
    w = w_in.astype(BF16)

    def regroup(cols):
        return cols.reshape(-1, RET_HEADS, 2, half).transpose(0, 2, 1, 3).reshape(-1, nqk)

    return jnp.concatenate([w[:, :s5w], regroup(w[:, s5w:s5w + nqk]),
                            regroup(w[:, s5w + nqk:s5w + 2 * nqk]), w[:, s5w + 2 * nqk:]], axis=1)


def _mixer_front(x, w_in, gn_gain, s5w, later_weights):
    bsz, seq, d = x.shape
    tl = TL_FRONT
    ret_w = RET_HEADS * RET_DV
    half = RET_DK // 2
    freqs = np.float32(ROPE_BASE) ** (-np.arange(half, dtype=np.float32) / np.float32(half))
    ang = (np.arange(seq, dtype=np.float32)[:, None] * freqs[None, :]).astype(np.float64)
    cos = jnp.asarray(np.tile(np.cos(ang), (1, RET_HEADS)), F32)
    sin = jnp.asarray(np.tile(np.sin(ang), (1, RET_HEADS)), F32)
    hm, dmask, xi_p, zk, gdec_p, smask = _retention_consts(TB_RET)
    w = _front_weight(w_in, s5w)
    rows = lambda a: pl.BlockSpec((a.shape[0] // bsz, a.shape[1]), lambda b, i: (b, 0))
    assert all(a.shape[0] % (bsz * SUBLANES_BF16) == 0 for a in later_weights)
    return pl.pallas_call(
        functools.partial(_front_kernel, len(later_weights)),
        grid=(bsz, seq // tl),
        in_specs=[
            pl.BlockSpec((1, tl, d), lambda b, i: (b, i, 0)),
            _const_spec(w.shape),
            pl.BlockSpec((tl, LANES), lambda b, i: (i, 0)),
            pl.BlockSpec((tl, LANES), lambda b, i: (i, 0)),
            _const_spec(hm.shape), _const_spec(dmask.shape), _const_spec(xi_p.shape),
            _const_spec(zk.shape), _const_spec(gdec_p.shape), _const_spec(smask.shape), _const_spec((1, ret_w)),
        ] + [rows(a) for a in later_weights],
        out_specs=[
            pl.BlockSpec((1, tl, s5w), lambda b, i: (b, i, 0)),
            pl.BlockSpec((1, tl, ret_w), lambda b, i: (b, i, 0)),
        ] + [rows(a) for a in later_weights],
        out_shape=[jax.ShapeDtypeStruct((bsz, seq, s5w), BF16),
                   jax.ShapeDtypeStruct((bsz, seq, ret_w), BF16)]
                  + [jax.ShapeDtypeStruct(a.shape, BF16) for a in later_weights],
        scratch_shapes=[pltpu.VMEM((RET_HEADS // 2, 2 * LANES, 2 * RET_DV), F32)],
        compiler_params=pltpu.CompilerParams(
            dimension_semantics=("arbitrary", "arbitrary"), vmem_limit_bytes=VMEM_LIMIT_BYTES),
        name="mixer_front",
    )(x, w, cos, sin, hm, dmask, xi_p, zk, gdec_p, smask, gn_gain.reshape(1, ret_w).astype(F32), *later_weights)


def _same_group(shape, row_shift, col_shift):
    gpb_mask = S5_BW // S5_GROUP - 1
    rows = lax.broadcasted_iota(jnp.int32, shape, 0) >> row_shift
    cols = (lax.broadcasted_iota(jnp.int32, shape, 1) >> col_shift) & gpb_mask
    return (rows == cols).astype(F32)


def _s5_build(kt_ref, at_ref, ct_ref, t_ref, e_ref, c_ref, h):
    q = kt_ref.shape[1]
    bw = S5_BW
    gpb = bw // S5_GROUP
    ch_shift = S5_GROUP.bit_length() - 1
    st_shift = S5_STATE.bit_length() - 1
    t_ref[h] = jnp.zeros(t_ref.shape[1:], t_ref.dtype)
    mask_t = _same_group((bw, bw), ch_shift, ch_shift)
    for lag in range(q):
        bd = (jnp.concatenate([kt_ref[h, lag]] * gpb, axis=0) * mask_t).astype(BF16)
        for i in range(q - lag):
            j = i + lag
            t_ref[h, i * bw:(i + 1) * bw, j * bw:(j + 1) * bw] = bd
    mask_e = _same_group((bw, e_ref.shape[2]), ch_shift, st_shift)
    for i in range(q):
        e_ref[h, i * bw:(i + 1) * bw, :] = (jnp.concatenate([at_ref[h, i]] * gpb, axis=0) * mask_e).astype(BF16)
    ns = c_ref.shape[1] // 2
    mask_c = _same_group((ns, c_ref.shape[2]), st_shift, ch_shift)
    for part in range(2):
        c_ref[h, part * ns:(part + 1) * ns, :] = (
            jnp.concatenate([ct_ref[h, part]] * gpb, axis=0) * mask_c).astype(BF16)


def _s5_kernel(u_ref, kt_ref, at_ref, ct_ref, aq_ref, y_ref, t_ref, e_ref, c_ref, sprev_ref, carry_ref):
    n_half = LANES // S5_BW

    @pl.when(pl.program_id(1) == 0)
    def _():
        carry_ref[...] = jnp.zeros_like(carry_ref)
        for h in range(n_half):
            _s5_build(kt_ref, at_ref, ct_ref, t_ref, e_ref, c_ref, h)

    nb, q, bsz, _ = u_ref.shape
    bw = S5_BW
    steps = [u_ref[:, j].reshape(nb * bsz, LANES) for j in range(q)]
    y_parts = []
    for h in range(n_half):
        xcat = jnp.concatenate([u[:, h * bw:(h + 1) * bw] for u in steps], axis=1)
        e = jnp.dot(xcat, e_ref[h], preferred_element_type=F32)
        ns = e.shape[1] // 2
        aqr = aq_ref[h, :, 0:ns]
        aqi = aq_ref[h, :, ns:2 * ns]
        s_re = carry_ref[h, :, 0:ns]
        s_im = carry_ref[h, :, ns:2 * ns]
        for n in range(nb):
            r0 = n * bsz
            sprev_ref[h, r0:r0 + bsz, :] = jnp.concatenate([s_re, s_im], axis=1).astype(BF16)
            e_re = e[r0:r0 + bsz, 0:ns]
            e_im = e[r0:r0 + bsz, ns:2 * ns]
            s_re, s_im = aqr * s_re - aqi * s_im + e_re, aqr * s_im + aqi * s_re + e_im
        carry_ref[h] = jnp.concatenate([s_re, s_im], axis=1)
        tw = 2 * LANES
        pieces = []
        for jt in range(q * bw // tw):
            hi = (jt + 1) * tw
            y = jnp.dot(xcat[:, 0:hi], t_ref[h, 0:hi, jt * tw:hi], preferred_element_type=F32)
            y = y + jnp.dot(sprev_ref[h], c_ref[h, :, jt * tw:hi], preferred_element_type=F32)
            pieces += [y[:, k * bw:(k + 1) * bw] for k in range(tw // bw)]
        y_parts.append(pieces)
    for j in range(q):
        yj = jnp.concatenate([y_parts[h][j] for h in range(n_half)], axis=1)
        y_ref[:, j] = yj.reshape(nb, bsz, LANES).astype(BF16)


def _s5_matrices(lam_re, lam_im, b_re, b_im, c_re, c_im, d, log_step, q):
    g, p = lam_re.shape
    cch = b_re.shape[-1]
    gpl = S5_BW // cch
    bw = gpl * cch
    nblk = g // gpl
    dt = jnp.exp(log_step)[:, None]
    zr, zi = lam_re * dt, lam_im * dt
    ks = jnp.arange(q + 1, dtype=F32)[:, None, None]
    mag = jnp.exp(ks * zr[None])
    pr, pi = mag * jnp.cos(ks * zi[None]), mag * jnp.sin(ks * zi[None])
    nr, ni = pr[1] - 1.0, pi[1]
    den = lam_re * lam_re + lam_im * lam_im
    fr = (nr * lam_re + ni * lam_im) / den
    fi = (ni * lam_re - nr * lam_im) / den
    hp = gpl * p
    lanes = lambda a: a.reshape(a.shape[:-2] + (nblk, hp))
    prb, pib, frb, fib = lanes(pr), lanes(pi), lanes(fr), lanes(fi)
    b_t = lambda b: b.reshape(nblk, gpl, p, cch).transpose(0, 3, 1, 2).reshape(nblk, cch, hp)
    c_t = lambda c: c.reshape(nblk, gpl, cch, p).transpose(0, 3, 1, 2)
    btr, bti = b_t(b_re), b_t(b_im)
    bbr = frb[:, None] * btr - fib[:, None] * bti
    bbi = frb[:, None] * bti + fib[:, None] * btr
    pre, pie = prb[q - 1::-1], pib[q - 1::-1]
    er = pre[:, :, None] * bbr[None] - pie[:, :, None] * bbi[None]
    ei = pre[:, :, None] * bbi[None] + pie[:, :, None] * bbr[None]
    at = jnp.concatenate([er, ei], axis=3).transpose(1, 0, 2, 3)
    cre, cim = c_re.reshape(nblk, gpl, cch, p), c_im.reshape(nblk, gpl, cch, p)
    er5, ei5 = er.reshape(q, nblk, cch, gpl, p), ei.reshape(q, nblk, cch, gpl, p)
    kern = (jnp.einsum('mhop,imchp->micho', cre, er5, precision=HI)
            - jnp.einsum('mhop,imchp->micho', cim, ei5, precision=HI))
    kt = kern[:, ::-1].reshape(nblk, q, cch, bw)
    skip = d.reshape(nblk, gpl, cch).transpose(0, 2, 1)[..., None] * jnp.eye(cch, dtype=F32)[None, :, None, :]
    kt = kt.at[:, 0].add(skip.reshape(nblk, cch, bw))
    crt, cit = c_t(c_re), c_t(c_im)
    prt = pr[1:].reshape(q, nblk, gpl, p).transpose(1, 3, 0, 2)[..., None]
    pit = pi[1:].reshape(q, nblk, gpl, p).transpose(1, 3, 0, 2)[..., None]
    cr = (crt[:, :, None] * prt - cit[:, :, None] * pit).reshape(nblk, p, q * bw)
    ci = (crt[:, :, None] * pit + cit[:, :, None] * prt).reshape(nblk, p, q * bw)
    ct = jnp.stack([cr, -ci], axis=1)
    aq = jnp.concatenate([prb[q], pib[q]], axis=1)
    return kt, at, ct, aq


def _s5_scan(u, mats):
    kt, at, ct, aq = mats
    bsz, seq, w = u.shape
    q, nb = S5_Q, S5_NB
    nblk = w // LANES
    n_half = LANES // S5_BW
    ns2 = aq.shape[-1]
    kw = q * S5_BW
    u4 = jnp.transpose(u, (1, 0, 2)).reshape(seq // q, q, bsz, w)
    aq_b = jnp.broadcast_to(aq[:, None, :], (nblk * n_half, bsz, ns2))
    blk = pl.BlockSpec((nb, q, bsz, LANES), lambda m, i: (i, 0, 0, m))
    tab = lambda a: pl.BlockSpec((n_half,) + a.shape[1:], lambda m, i: (m,) + (0,) * (a.ndim - 1))
    y4 = pl.pallas_call(
        _s5_kernel,
        grid=(nblk, seq // (q * nb)),
        in_specs=[blk, tab(kt), tab(at), tab(ct), tab(aq_b)],
        out_specs=blk,
        out_shape=jax.ShapeDtypeStruct(u4.shape, BF16),
        scratch_shapes=[pltpu.VMEM((n_half, kw, kw), BF16),
                        pltpu.VMEM((n_half, kw, ns2), BF16),
                        pltpu.VMEM((n_half, ns2, kw), BF16),
                        pltpu.VMEM((n_half, nb * bsz, ns2), BF16),
                        pltpu.VMEM((n_half, bsz, ns2), F32)],
        compiler_params=pltpu.CompilerParams(
            dimension_semantics=("arbitrary", "arbitrary"), vmem_limit_bytes=VMEM_LIMIT_BYTES),
        name="s5_scan",
    )(u4, kt, at, ct, aq_b)
    return jnp.transpose(y4.reshape(seq, bsz, w), (1, 0, 2))


def _ffn_kernel(alpha, x_ref, ys_ref, yr_ref, gluw_ref, glub_ref, s5g_ref, wout_ref, ln1g_ref, ln1b_ref,
                wup_ref, cw_ref, cb_ref, wdn_ref, ln2g_ref, ln2b_ref, o_ref, xb_ref, hmid_ref, tail_ref):
    @pl.when(pl.program_id(1) == 0)
    def _():
        tail_ref[...] = jnp.zeros_like(tail_ref)

    s5w = ys_ref.shape[-1]
    dff = hmid_ref.shape[1]
    halo = tail_ref.shape[0]

    def mix(r0, ts):
        y = ys_ref[0, r0:r0 + ts, :].astype(F32)
        cdf = 0.5 * (1.0 + jnp.tanh(math.sqrt(2.0 / math.pi) * (y + 0.044715 * (y * y * y))))
        y = y * cdf
        z = jnp.dot(y.astype(BF16), gluw_ref[...], preferred_element_type=F32) + glub_ref[...]
        y = y * _sigmoid(z)
        y = y * lax.rsqrt(jnp.mean(y * y, axis=-1, keepdims=True) + LN_EPS) * s5g_ref[...]
        mix = jnp.dot(y.astype(BF16), wout_ref[0:s5w, :], preferred_element_type=F32)
        mix = mix + jnp.dot(yr_ref[0, r0:r0 + ts, :], wout_ref[s5w:, :], preferred_element_type=F32)
        x1 = _layer_norm(alpha * x_ref[0, r0:r0 + ts, :] + mix, ln1g_ref[...], ln1b_ref[...])
        o_ref[0, r0:r0 + ts, :] = x1
        xb_ref[r0:r0 + ts, :] = x1.astype(BF16)

    def chunk(r0, ts, c):
        xb = xb_ref[r0:r0 + ts, :]
        c0 = c * FC_FFN
        c1 = min(c0 + FC_FFN, dff)
        a = jnp.dot(xb, wup_ref[:, c0:c1], preferred_element_type=F32)
        g = jnp.dot(xb, wup_ref[:, dff + c0:dff + c1], preferred_element_type=F32)
        ext = jnp.concatenate([tail_ref[:, c0:c1], a], axis=0)
        tail_ref[:, c0:c1] = a[ts - halo:ts]
        a1 = ext[halo - 1:halo - 1 + ts]
        a2 = ext[halo - 2:halo - 2 + ts]
        cv = cw_ref[2:3, c0:c1] * a + cw_ref[1:2, c0:c1] * a1 + cw_ref[0:1, c0:c1] * a2 + cb_ref[:, c0:c1]
        hmid_ref[r0:r0 + ts, c0:c1] = (cv * _sigmoid(cv) * g).astype(BF16)

    def down(r0, ts):
        ffn = jnp.dot(hmid_ref[r0:r0 + ts, :], wdn_ref[...], preferred_element_type=F32)
        o_ref[0, r0:r0 + ts, :] = alpha * o_ref[0, r0:r0 + ts, :] + ffn

    def out(r0, ts):
        o_ref[0, r0:r0 + ts, :] = _layer_norm(o_ref[0, r0:r0 + ts, :], ln2g_ref[...], ln2b_ref[...])

    starts = [sum(FFN_SUBS[:i]) for i in range(len(FFN_SUBS))]
    mix(0, FFN_SUBS[0])
    for sub, (r0, ts) in enumerate(zip(starts, FFN_SUBS)):
        for c in range(pl.cdiv(dff, FC_FFN)):
            chunk(r0, ts, c)
            if c == FFN_MIX_AT and sub + 1 < len(FFN_SUBS):
                mix(r0 + ts, FFN_SUBS[sub + 1])
            if c == FFN_OUT_AT and sub > 0:
                out(starts[sub - 1], FFN_SUBS[sub - 1])
        down(r0, ts)
    out(starts[-1], FFN_SUBS[-1])


def _mixer_ffn(alpha, x, ys, yr, glu_w, glu_b, s5_gain, w_out, ln1_g, ln1_b,
               w_up, conv_w, conv_b, w_down, ln2_g, ln2_b):
    bsz, seq, d = x.shape
    tm = TM_FFN
    assert sum(FFN_SUBS) == tm and max(FFN_MIX_AT, FFN_OUT_AT) < pl.cdiv(w_down.shape[0], FC_FFN)
    s5w, ret_w = ys.shape[-1], yr.shape[-1]
    dff = w_down.shape[0]
    row = lambda a: a.reshape(1, -1).astype(F32)
    tok = lambda w: pl.BlockSpec((1, tm, w), lambda b, i: (b, i, 0))
    assert all(w.dtype == BF16 for w in (glu_w, w_out, w_up, w_down))
    operands = [x, ys, yr, glu_w, row(glu_b), row(s5_gain), w_out, row(ln1_g), row(ln1_b), w_up,
                conv_w.astype(F32), row(conv_b), w_down, row(ln2_g), row(ln2_b)]
    in_specs = [tok(d), tok(s5w), tok(ret_w)] + [_const_spec(a.shape) for a in operands[3:]]
    return pl.pallas_call(
        functools.partial(_ffn_kernel, alpha),
        grid=(bsz, seq // tm),
        in_specs=in_specs,
        out_specs=tok(d),
        out_shape=jax.ShapeDtypeStruct((bsz, seq, d), x.dtype),
        scratch_shapes=[pltpu.VMEM((tm, d), BF16), pltpu.VMEM((tm, dff), BF16), pltpu.VMEM((8, dff), F32)],
        compiler_params=pltpu.CompilerParams(
            dimension_semantics=("arbitrary", "arbitrary"), vmem_limit_bytes=VMEM_LIMIT_BYTES),
        name="mixer_ffn",
    )(*operands)


def kernel(x, w_in, s5_lambda_re, s5_lambda_im, s5_b_re, s5_b_im, s5_c_re, s5_c_im, s5_d, s5_log_step,
           s5_glu_w, s5_glu_b, s5_out_gain, ret_gn_gain, w_out, ln1_g, ln1_b, ffn_w_up, ffn_conv_w,
           ffn_conv_b, ffn_w_down, ln2_g, ln2_b):
    depth = w_in.shape[0]
    alpha = (2.0 * depth) ** 0.25
    s5w = s5_d.shape[-1]
    for l in range(depth):
        u, yr, glu_w, w_o, w_up, w_down = _mixer_front(
            x, w_in[l], ret_gn_gain[l], s5w, [s5_glu_w[l], w_out[l], ffn_w_up[l], ffn_w_down[l]])
        mats = _s5_matrices(s5_lambda_re[l], s5_lambda_im[l], s5_b_re[l], s5_b_im[l], s5_c_re[l],
                            s5_c_im[l], s5_d[l], s5_log_step[l], S5_Q)
        ys = _s5_scan(u, mats)
        x = _mixer_ffn(alpha, x, ys, yr, glu_w, s5_glu_b[l], s5_out_gain[l], w_o, ln1_g[l],
                       ln1_b[l], w_up, ffn_conv_w[l], ffn_conv_b[l], w_down, ln2_g[l], ln2_b[l])
    return x
```

```python
import functools
import math

import numpy as np
import jax
import jax.numpy as jnp
from jax import lax
from jax.experimental import pallas as pl
from jax.experimental.pallas import tpu as pltpu

CHUNK = 64
S5_GROUP = 16
S5_STATE = 64
RET_HEADS = 4
RET_DK = 64
RET_DV = 128
CONV_W = 3
ROPE_BASE = 10000.0
LN_EPS = 1e-5

LANES = 128
SUBLANES_BF16 = 16
VMEM_LIMIT_BYTES = 56 * 1024 * 1024

TL_FRONT = 2048
TB_RET = 256
S5_Q = 8
S5_BW = 64
S5_NB = 128
TM_FFN = 512
FC_FFN = 256
FFN_SUBS = (256, 256)
FFN_MIX_AT = 1
FFN_OUT_AT = 6

F32 = jnp.float32
BF16 = jnp.bfloat16
HI = lax.Precision.HIGHEST


def _const_spec(shape):
    nd = len(shape)
    return pl.BlockSpec(shape, lambda *_: (0,) * nd, pipeline_mode=pl.Buffered(1))


def _layer_norm(v, g, b):
    mu = jnp.mean(v, axis=-1, keepdims=True)
    vc = v - mu
    var = jnp.mean(vc * vc, axis=-1, keepdims=True)
    return vc * lax.rsqrt(var + LN_EPS) * g + b


def _sigmoid(v):
    return 1.0 / (1.0 + jnp.exp(-v))


def _front_kernel(n_cast, x_ref, w_ref, cos_ref, sin_ref, hm_ref, dmask_ref, xi_ref, zk_ref,
                  gdec_ref, smask_ref, gn_ref, *rest):
    cast_in = rest[:n_cast]
    u_ref, yr_ref = rest[n_cast:n_cast + 2]
    cast_out = rest[n_cast + 2:2 * n_cast + 2]
    state_ref = rest[2 * n_cast + 2]
    @pl.when(pl.program_id(1) == 0)
    def _():
        state_ref[...] = jnp.zeros_like(state_ref)

    s5w = u_ref.shape[-1]
    xb = x_ref[0].astype(BF16)
    o_qk = s5w
    o_v = o_qk + 4 * LANES
    ret_w = RET_HEADS * RET_DV
    pw = 2 * RET_DV

    def project(r0):
        xr = xb[r0:r0 + TB_RET]
        qk = jnp.dot(xr, w_ref[:, o_qk:o_qk + 4 * LANES], preferred_element_type=F32)
        cos = cos_ref[r0:r0 + TB_RET, :]
        sin = sin_ref[r0:r0 + TB_RET, :]
        q1, q2 = qk[:, 0:LANES], qk[:, LANES:2 * LANES]
        k1, k2 = qk[:, 2 * LANES:3 * LANES], qk[:, 3 * LANES:4 * LANES]
        qb = jnp.concatenate([q1 * cos - q2 * sin, q1 * sin + q2 * cos], axis=1).astype(BF16)
        kb = jnp.concatenate([k1 * cos - k2 * sin, k1 * sin + k2 * cos], axis=1)
        v = jnp.dot(xr, w_ref[:, o_v:o_v + ret_w], preferred_element_type=F32).astype(BF16)
        gate = jnp.dot(xr, w_ref[:, o_v + ret_w:o_v + 2 * ret_w], preferred_element_type=F32)
        return qb, kb, v, gate

    def retain(r0, qb, kb, v, gate):
        kbb = kb.astype(BF16)
        for pair in range(RET_HEADS // 2):
            p0 = pair * pw
            st = state_ref[pair]
            cross = xi_ref[pair] * jnp.dot(qb, st.astype(BF16), preferred_element_type=F32)
            kz = (kb * zk_ref[pair]).astype(BF16)
            upd = lax.dot_general(kz, v[:, p0:p0 + pw], (((0,), (0,)), ((), ())), preferred_element_type=F32)
            state_ref[pair] = st * gdec_ref[pair] + upd * smask_ref[pair]
            for sub in range(2):
                hd = 2 * pair + sub
                c0 = hd * RET_DV
                kh = kbb * hm_ref[hd]
                sc = lax.dot_general(qb, kh, (((1,), (1,)), ((), ())), preferred_element_type=F32)
                p = (sc * dmask_ref[hd]).astype(BF16)
                o = jnp.dot(p, v[:, c0:c0 + RET_DV], preferred_element_type=F32)
                o = o + cross[:, sub * RET_DV:(sub + 1) * RET_DV]
                mu = jnp.mean(o, axis=-1, keepdims=True)
                oc = o - mu
                var = jnp.mean(oc * oc, axis=-1, keepdims=True)
                on = oc * lax.rsqrt(var + LN_EPS) * gn_ref[:, c0:c0 + RET_DV]
                g = gate[:, c0:c0 + RET_DV]
                yr_ref[0, r0:r0 + TB_RET, c0:c0 + RET_DV] = (on * (g * _sigmoid(g))).astype(BF16)

    n_blk = xb.shape[0] // TB_RET
    cur = project(0)
    for blk in range(n_blk):
        nxt = project((blk + 1) * TB_RET) if blk + 1 < n_blk else None
        retain(blk * TB_RET, *cur)
        cur = nxt

    u_ref[0] = jnp.dot(xb, w_ref[:, 0:s5w], preferred_element_type=F32).astype(BF16)

    for src, dst in zip(cast_in, cast_out):
        dst[...] = src[...].astype(BF16)


def _retention_consts(tb):
    log_gamma = np.log1p(-(2.0 ** (-5.0 - np.arange(RET_HEADS, dtype=np.float64))))
    pos = np.arange(tb, dtype=np.float64)
    chunk = np.arange(tb) // CHUNK
    visible = (chunk[None, :] <= chunk[:, None]).astype(np.float64)
    dist = np.abs(pos[:, None] - pos[None, :])
    dmask = np.exp(dist[None] * log_gamma[:, None, None]) * visible[None]
    xi = np.exp((pos + 1.0)[None, :] * log_gamma[:, None])
    zeta = np.exp((tb - 1.0 - pos)[None, :] * log_gamma[:, None])
    gdec = np.exp(tb * log_gamma)
    lane = np.arange(2 * LANES) % LANES
    half = RET_DK // 2
    hm = np.stack([(lane // half == h) for h in range(RET_HEADS)]).astype(np.float64) * RET_DK ** -0.5
    npair = RET_HEADS // 2
    own = (hm > 0).astype(np.float64)
    xi_p = np.repeat(xi[:, :, None], RET_DV, axis=2).reshape(npair, 2, tb, RET_DV)
    xi_p = np.concatenate([xi_p[:, 0], xi_p[:, 1]], axis=2)
    zk = (zeta[:, :, None] * hm[:, None, :]).reshape(npair, 2, tb, 2 * LANES).sum(axis=1)
    col_head = np.repeat(np.arange(RET_HEADS).reshape(npair, 2), RET_DV, axis=1)
    smask = np.stack([own[col_head[p]].T for p in range(npair)])
    gdec_p = smask * gdec[col_head][:, None, :]
    tables = tuple(jnp.asarray(np.ascontiguousarray(a), F32) for a in (dmask, xi_p, zk, gdec_p, smask))
    return (jnp.asarray(hm[:, None, :], BF16),) + tables


def _front_weight(w_in, s5w):
    half = RET_DK // 2
    nqk = RET_HEADS * RET_DK
    w = w_in.astype(BF16)

    def regroup(cols):
        return cols.reshape(-1, RET_HEADS, 2, half).transpose(0, 2, 1, 3).reshape(-1, nqk)

    return jnp.concatenate([w[:, :s5w], regroup(w[:, s5w:s5w + nqk]),
                            regroup(w[:, s5w + nqk:s5w + 2 * nqk]), w[:, s5w + 2 * nqk:]], axis=1)


def _mixer_front(x, w_in, gn_gain, s5w, later_weights):
    bsz, seq, d = x.shape
    tl = TL_FRONT
    ret_w = RET_HEADS * RET_DV
    half = RET_DK // 2
    freqs = np.float32(ROPE_BASE) ** (-np.arange(half, dtype=np.float32) / np.float32(half))
    ang = (np.arange(seq, dtype=np.float32)[:, None] * freqs[None, :]).astype(np.float64)
    cos = jnp.asarray(np.tile(np.cos(ang), (1, RET_HEADS)), F32)
    sin = jnp.asarray(np.tile(np.sin(ang), (1, RET_HEADS)), F32)
    hm, dmask, xi_p, zk, gdec_p, smask = _retention_consts(TB_RET)
    w = _front_weight(w_in, s5w)
    rows = lambda a: pl.BlockSpec((a.shape[0] // bsz, a.shape[1]), lambda b, i: (b, 0))
    assert all(a.shape[0] % (bsz * SUBLANES_BF16) == 0 for a in later_weights)
    return pl.pallas_call(
        functools.partial(_front_kernel, len(later_weights)),
        grid=(bsz, seq // tl),
        in_specs=[
            pl.BlockSpec((1, tl, d), lambda b, i: (b, i, 0)),
            _const_spec(w.shape),
            pl.BlockSpec((tl, LANES), lambda b, i: (i, 0)),
            pl.BlockSpec((tl, LANES), lambda b, i: (i, 0)),
            _const_spec(hm.shape), _const_spec(dmask.shape), _const_spec(xi_p.shape),
            _const_spec(zk.shape), _const_spec(gdec_p.shape), _const_spec(smask.shape), _const_spec((1, ret_w)),
        ] + [rows(a) for a in later_weights],
        out_specs=[
            pl.BlockSpec((1, tl, s5w), lambda b, i: (b, i, 0)),
            pl.BlockSpec((1, tl, ret_w), lambda b, i: (b, i, 0)),
        ] + [rows(a) for a in later_weights],
        out_shape=[jax.ShapeDtypeStruct((bsz, seq, s5w), BF16),
                   jax.ShapeDtypeStruct((bsz, seq, ret_w), BF16)]
                  + [jax.ShapeDtypeStruct(a.shape, BF16) for a in later_weights],
        scratch_shapes=[pltpu.VMEM((RET_HEADS // 2, 2 * LANES, 2 * RET_DV), F32)],
        compiler_params=pltpu.CompilerParams(
            dimension_semantics=("arbitrary", "arbitrary"), vmem_limit_bytes=VMEM_LIMIT_BYTES),
        name="mixer_front",
    )(x, w, cos, sin, hm, dmask, xi_p, zk, gdec_p, smask, gn_gain.reshape(1, ret_w).astype(F32), *later_weights)


def _same_group(shape, row_shift, col_shift):
    gpb_mask = S5_BW // S5_GROUP - 1
    rows = lax.broadcasted_iota(jnp.int32, shape, 0) >> row_shift
    cols = (lax.broadcasted_iota(jnp.int32, shape, 1) >> col_shift) & gpb_mask
    return (rows == cols).astype(F32)


def _s5_build(kt_ref, at_ref, ct_ref, t_ref, e_ref, c_ref, h):
    q = kt_ref.shape[1]
    bw = S5_BW
    gpb = bw // S5_GROUP
    ch_shift = S5_GROUP.bit_length() - 1
    st_shift = S5_STATE.bit_length() - 1
    t_ref[h] = jnp.zeros(t_ref.shape[1:], t_ref.dtype)
    mask_t = _same_group((bw, bw), ch_shift, ch_shift)
    for lag in range(q):
        bd = (jnp.concatenate([kt_ref[h, lag]] * gpb, axis=0) * mask_t).astype(BF16)
        for i in range(q - lag):
            j = i + lag
            t_ref[h, i * bw:(i + 1) * bw, j * bw:(j + 1) * bw] = bd
    mask_e = _same_group((bw, e_ref.shape[2]), ch_shift, st_shift)
    for i in range(q):
        e_ref[h, i * bw:(i + 1) * bw, :] = (jnp.concatenate([at_ref[h, i]] * gpb, axis=0) * mask_e).astype(BF16)
    ns = c_ref.shape[1] // 2
    mask_c = _same_group((ns, c_ref.shape[2]), st_shift, ch_shift)
    for part in range(2):
        c_ref[h, part * ns:(part + 1) * ns, :] = (
            jnp.concatenate([ct_ref[h, part]] * gpb, axis=0) * mask_c).astype(BF16)


def _s5_kernel(u_ref, kt_ref, at_ref, ct_ref, aq_ref, y_ref, t_ref, e_ref, c_ref, sprev_ref, carry_ref):
    n_half = LANES // S5_BW

    @pl.when(pl.program_id(1) == 0)
    def _():
        carry_ref[...] = jnp.zeros_like(carry_ref)
        for h in range(n_half):
            _s5_build(kt_ref, at_ref, ct_ref, t_ref, e_ref, c_ref, h)

    nb, q, bsz, _ = u_ref.shape
    bw = S5_BW
    steps = [u_ref[:, j].reshape(nb * bsz, LANES) for j in range(q)]
    y_parts = []
    for h in range(n_half):
        xcat = jnp.concatenate([u[:, h * bw:(h + 1) * bw] for u in steps], axis=1)
        e = jnp.dot(xcat, e_ref[h], preferred_element_type=F32)
        ns = e.shape[1] // 2
        aqr = aq_ref[h, :, 0:ns]
        aqi = aq_ref[h, :, ns:2 * ns]
        s_re = carry_ref[h, :, 0:ns]
        s_im = carry_ref[h, :, ns:2 * ns]
        for n in range(nb):
            r0 = n * bsz
            sprev_ref[h, r0:r0 + bsz, :] = jnp.concatenate([s_re, s_im], axis=1).astype(BF16)
            e_re = e[r0:r0 + bsz, 0:ns]
            e_im = e[r0:r0 + bsz, ns:2 * ns]
            s_re, s_im = aqr * s_re - aqi * s_im + e_re, aqr * s_im + aqi * s_re + e_im
        carry_ref[h] = jnp.concatenate([s_re, s_im], axis=1)
        tw = 2 * LANES
        pieces = []
        for jt in range(q * bw // tw):
            hi = (jt + 1) * tw
            y = jnp.dot(xcat[:, 0:hi], t_ref[h, 0:hi, jt * tw:hi], preferred_element_type=F32)
            y = y + jnp.dot(sprev_ref[h], c_ref[h, :, jt * tw:hi], preferred_element_type=F32)
            pieces += [y[:, k * bw:(k + 1) * bw] for k in range(tw // bw)]
        y_parts.append(pieces)
    for j in range(q):
        yj = jnp.concatenate([y_parts[h][j] for h in range(n_half)], axis=1)
        y_ref[:, j] = yj.reshape(nb, bsz, LANES).astype(BF16)


def _s5_matrices(lam_re, lam_im, b_re, b_im, c_re, c_im, d, log_step, q):
    g, p = lam_re.shape
    cch = b_re.shape[-1]
    gpl = S5_BW // cch
    bw = gpl * cch
    nblk = g // gpl
    dt = jnp.exp(log_step)[:, None]
    zr, zi = lam_re * dt, lam_im * dt
    ks = jnp.arange(q + 1, dtype=F32)[:, None, None]
    mag = jnp.exp(ks * zr[None])
    pr, pi = mag * jnp.cos(ks * zi[None]), mag * jnp.sin(ks * zi[None])
    nr, ni = pr[1] - 1.0, pi[1]
    den = lam_re * lam_re + lam_im * lam_im
    fr = (nr * lam_re + ni * lam_im) / den
    fi = (ni * lam_re - nr * lam_im) / den
    hp = gpl * p
    lanes = lambda a: a.reshape(a.shape[:-2] + (nblk, hp))
    prb, pib, frb, fib = lanes(pr), lanes(pi), lanes(fr), lanes(fi)
    b_t = lambda b: b.reshape(nblk, gpl, p, cch).transpose(0, 3, 1, 2).reshape(nblk, cch, hp)
    c_t = lambda c: c.reshape(nblk, gpl, cch, p).transpose(0, 3, 1, 2)
    btr, bti = b_t(b_re), b_t(b_im)
    bbr = frb[:, None] * btr - fib[:, None] * bti
    bbi = frb[:, None] * bti + fib[:, None] * btr
    pre, pie = prb[q - 1::-1], pib[q - 1::-1]
    er = pre[:, :, None] * bbr[None] - pie[:, :, None] * bbi[None]
    ei = pre[:, :, None] * bbi[None] + pie[:, :, None] * bbr[None]
    at = jnp.concatenate([er, ei], axis=3).transpose(1, 0, 2, 3)
    cre, cim = c_re.reshape(nblk, gpl, cch, p), c_im.reshape(nblk, gpl, cch, p)
    er5, ei5 = er.reshape(q, nblk, cch, gpl, p), ei.reshape(q, nblk, cch, gpl, p)
    kern = (jnp.einsum('mhop,imchp->micho', cre, er5, precision=HI)
            - jnp.einsum('mhop,imchp->micho', cim, ei5, precision=HI))
    kt = kern[:, ::-1].reshape(nblk, q, cch, bw)
    skip = d.reshape(nblk, gpl, cch).transpose(0, 2, 1)[..., None] * jnp.eye(cch, dtype=F32)[None, :, None, :]
    kt = kt.at[:, 0].add(skip.reshape(nblk, cch, bw))
    crt, cit = c_t(c_re), c_t(c_im)
    prt = pr[1:].reshape(q, nblk, gpl, p).transpose(1, 3, 0, 2)[..., None]
    pit = pi[1:].reshape(q, nblk, gpl, p).transpose(1, 3, 0, 2)[..., None]
    cr = (crt[:, :, None] * prt - cit[:, :, None] * pit).reshape(nblk, p, q * bw)
    ci = (crt[:, :, None] * pit + cit[:, :, None] * prt).reshape(nblk, p, q * bw)
    ct = jnp.stack([cr, -ci], axis=1)
    aq = jnp.concatenate([prb[q], pib[q]], axis=1)
    return kt, at, ct, aq


def _s5_scan(u, mats):
    kt, at, ct, aq = mats
    bsz, seq, w = u.shape
    q, nb = S5_Q, S5_NB
    nblk = w // LANES
    n_half = LANES // S5_BW
    ns2 = aq.shape[-1]
    kw = q * S5_BW
    u4 = jnp.transpose(u, (1, 0, 2)).reshape(seq // q, q, bsz, w)
    aq_b = jnp.broadcast_to(aq[:, None, :], (nblk * n_half, bsz, ns2))
    blk = pl.BlockSpec((nb, q, bsz, LANES), lambda m, i: (i, 0, 0, m))
    tab = lambda a: pl.BlockSpec((n_half,) + a.shape[1:], lambda m, i: (m,) + (0,) * (a.ndim - 1))
    y4 = pl.pallas_call(
        _s5_kernel,
        grid=(nblk, seq // (q * nb)),
        in_specs=[blk, tab(kt), tab(at), tab(ct), tab(aq_b)],
        out_specs=blk,
        out_shape=jax.ShapeDtypeStruct(u4.shape, BF16),
        scratch_shapes=[pltpu.VMEM((n_half, kw, kw), BF16),
                        pltpu.VMEM((n_half, kw, ns2), BF16),
                        pltpu.VMEM((n_half, ns2, kw), BF16),
                        pltpu.VMEM((n_half, nb * bsz, ns2), BF16),
                        pltpu.VMEM((n_half, bsz, ns2), F32)],
        compiler_params=pltpu.CompilerParams(
            dimension_semantics=("arbitrary", "arbitrary"), vmem_limit_bytes=VMEM_LIMIT_BYTES),
        name="s5_scan",
    )(u4, kt, at, ct, aq_b)
    return jnp.transpose(y4.reshape(seq, bsz, w), (1, 0, 2))


def _ffn_kernel(alpha, x_ref, ys_ref, yr_ref, gluw_ref, glub_ref, s5g_ref, wout_ref, ln1g_ref, ln1b_ref,
                wup_ref, cw_ref, cb_ref, wdn_ref, ln2g_ref, ln2b_ref, o_ref, xb_ref, hmid_ref, tail_ref):
    @pl.when(pl.program_id(1) == 0)
    def _():
        tail_ref[...] = jnp.zeros_like(tail_ref)

    s5w = ys_ref.shape[-1]
    dff = hmid_ref.shape[1]
    halo = tail_ref.shape[0]

    def mix(r0, ts):
        y = ys_ref[0, r0:r0 + ts, :].astype(F32)
        cdf = 0.5 * (1.0 + jnp.tanh(math.sqrt(2.0 / math.pi) * (y + 0.044715 * (y * y * y))))
        y = y * cdf
        z = jnp.dot(y.astype(BF16), gluw_ref[...], preferred_element_type=F32) + glub_ref[...]
        y = y * _sigmoid(z)
        y = y * lax.rsqrt(jnp.mean(y * y, axis=-1, keepdims=True) + LN_EPS) * s5g_ref[...]
        mix = jnp.dot(y.astype(BF16), wout_ref[0:s5w, :], preferred_element_type=F32)
        mix = mix + jnp.dot(yr_ref[0, r0:r0 + ts, :], wout_ref[s5w:, :], preferred_element_type=F32)
        x1 = _layer_norm(alpha * x_ref[0, r0:r0 + ts, :] + mix, ln1g_ref[...], ln1b_ref[...])
        o_ref[0, r0:r0 + ts, :] = x1
        xb_ref[r0:r0 + ts, :] = x1.astype(BF16)

    def chunk(r0, ts, c):
        xb = xb_ref[r0:r0 + ts, :]
        c0 = c * FC_FFN
        c1 = min(c0 + FC_FFN, dff)
        a = jnp.dot(xb, wup_ref[:, c0:c1], preferred_element_type=F32)
        g = jnp.dot(xb, wup_ref[:, dff + c0:dff + c1], preferred_element_type=F32)
        ext = jnp.concatenate([tail_ref[:, c0:c1], a], axis=0)
        tail_ref[:, c0:c1] = a[ts - halo:ts]
        a1 = ext[halo - 1:halo - 1 + ts]
        a2 = ext[halo - 2:halo - 2 + ts]
        cv = cw_ref[2:3, c0:c1] * a + cw_ref[1:2, c0:c1] * a1 + cw_ref[0:1, c0:c1] * a2 + cb_ref[:, c0:c1]
        hmid_ref[r0:r0 + ts, c0:c1] = (cv * _sigmoid(cv) * g).astype(BF16)

    def down(r0, ts):
        ffn = jnp.dot(hmid_ref[r0:r0 + ts, :], wdn_ref[...], preferred_element_type=F32)
        o_ref[0, r0:r0 + ts, :] = alpha * o_ref[0, r0:r0 + ts, :] + ffn

    def out(r0, ts):
        o_ref[0, r0:r0 + ts, :] = _layer_norm(o_ref[0, r0:r0 + ts, :], ln2g_ref[...], ln2b_ref[...])

    starts = [sum(FFN_SUBS[:i]) for i in range(len(FFN_SUBS))]
    mix(0, FFN_SUBS[0])
    for sub, (r0, ts) in enumerate(zip(starts, FFN_SUBS)):
        for c in range(pl.cdiv(dff, FC_FFN)):
            chunk(r0, ts, c)
            if c == FFN_MIX_AT and sub + 1 < len(FFN_SUBS):
                mix(r0 + ts, FFN_SUBS[sub + 1])
            if c == FFN_OUT_AT and sub > 0:
                out(starts[sub - 1], FFN_SUBS[sub - 1])
        down(r0, ts)
    out(starts[-1], FFN_SUBS[-1])


def _mixer_ffn(alpha, x, ys, yr, glu_w, glu_b, s5_gain, w_out, ln1_g, ln1_b,
               w_up, conv_w, conv_b, w_down, ln2_g, ln2_b):
    bsz, seq, d = x.shape
    tm = TM_FFN
    assert sum(FFN_SUBS) == tm and max(FFN_MIX_AT, FFN_OUT_AT) < pl.cdiv(w_down.shape[0], FC_FFN)
    s5w, ret_w = ys.shape[-1], yr.shape[-1]
    dff = w_down.shape[0]
    row = lambda a: a.reshape(1, -1).astype(F32)
    tok = lambda w: pl.BlockSpec((1, tm, w), lambda b, i: (b, i, 0))
    assert all(w.dtype == BF16 for w in (glu_w, w_out, w_up, w_down))
    operands = [x, ys, yr, glu_w, row(glu_b), row(s5_gain), w_out, row(ln1_g), row(ln1_b), w_up,
                conv_w.astype(F32), row(conv_b), w_down, row(ln2_g), row(ln2_b)]
    in_specs = [tok(d), tok(s5w), tok(ret_w)] + [_const_spec(a.shape) for a in operands[3:]]
    return pl.pallas_call(
        functools.partial(_ffn_kernel, alpha),
        grid=(bsz, seq // tm),
        in_specs=in_specs,
        out_specs=tok(d),
        out_shape=jax.ShapeDtypeStruct((bsz, seq, d), x.dtype),
        scratch_shapes=[pltpu.VMEM((tm, d), BF16), pltpu.VMEM((tm, dff), BF16), pltpu.VMEM((8, dff), F32)],
        compiler_params=pltpu.CompilerParams(
            dimension_semantics=("arbitrary", "arbitrary"), vmem_limit_bytes=VMEM_LIMIT_BYTES),
        name="mixer_ffn",
    )(*operands)


def kernel(x, w_in, s5_lambda_re, s5_lambda_im, s5_b_re, s5_b_im, s5_c_re, s5_c_im, s5_d, s5_log_step,
           s5_glu_w, s5_glu_b, s5_out_gain, ret_gn_gain, w_out, ln1_g, ln1_b, ffn_w_up, ffn_conv_w,
           ffn_conv_b, ffn_w_down, ln2_g, ln2_b):
    depth = w_in.shape[0]
    alpha = (2.0 * depth) ** 0.25
    s5w = s5_d.shape[-1]
    for l in range(depth):
        u, yr, glu_w, w_o, w_up, w_down = _mixer_front(
            x, w_in[l], ret_gn_gain[l], s5w, [s5_glu_w[l], w_out[l], ffn_w_up[l], ffn_w_down[l]])
        mats = _s5_matrices(s5_lambda_re[l], s5_lambda_im[l], s5_b_re[l], s5_b_im[l], s5_c_re[l],
                            s5_c_im[l], s5_d[l], s5_log_step[l], S5_Q)
        ys = _s5_scan(u, mats)
        x = _mixer_ffn(alpha, x, ys, yr, glu_w, s5_glu_b[l], s5_out_gain[l], w_o, ln1_g[l],
                       ln1_b[l], w_up, ffn_conv_w[l], ffn_conv_b[l], w_down, ln2_g[l], ln2_b[l])
    return x
```
